```python
import math
import jax, jax.numpy as jnp
from jax import lax
import numpy as np

D_MODEL = 1024
BATCH = 8
SEQ = 4096
DEPTH = 4

GRID_W = 64
CTX_LEN = 256
EPS = 1e-6

MLA_HEADS = 4
MLA_Q_LORA = 256
MLA_KV_LORA = 128
MLA_NOPE = 128
MLA_ROPE = 64
MLA_V = 128
MLA_WIDTH = MLA_HEADS * MLA_V
MLA_SCALE = (MLA_NOPE + MLA_ROPE) ** -0.5
ROPE_BASE = 10000.0
Q_BLOCK = 128

CM_GROUPS = 4
CM_WIDTH = 256
CM_GROUP_DIM = CM_WIDTH // CM_GROUPS
CM_CHUNK = 128

SSD_WIDTH = 256
SSD_HEAD_DIM = 64
SSD_HEADS = SSD_WIDTH // SSD_HEAD_DIM
SSD_GROUPS = 2
SSD_STATE = 128
SSD_CONV = 3
SSD_CHUNK = 128
SSD_CONV_DIM = SSD_WIDTH + 2 * SSD_GROUPS * SSD_STATE

MIX_WIDTH = MLA_WIDTH + CM_WIDTH + SSD_WIDTH
D_FF = 4 * D_MODEL

IN_PARTS = [MLA_Q_LORA, MLA_KV_LORA, MLA_ROPE, 2 * CM_WIDTH, SSD_WIDTH, SSD_CONV_DIM, 2 * SSD_HEADS]
IN_WIDTH = sum(IN_PARTS)
IN_SPLIT_IDX = [int(i) for i in np.cumsum(IN_PARTS)[:-1]]

kernel_name = 'hybrid_mla_chunkmlp_ssd_dit_trunk'


def rmsnorm(x, g):
    xf = x.astype(jnp.float32)
    y = xf * lax.rsqrt(jnp.mean(xf * xf, axis=-1, keepdims=True) + EPS)
    return (y * g.astype(jnp.float32)).astype(x.dtype)


def layernorm(x, g):
    xf = x.astype(jnp.float32)
    mu = jnp.mean(xf, axis=-1, keepdims=True)
    var = jnp.mean(jnp.square(xf - mu), axis=-1, keepdims=True)
    return ((xf - mu) * lax.rsqrt(var + EPS) * g.astype(jnp.float32)).astype(x.dtype)


def modulate(h, shift, scale):
    return h * (1 + scale) + shift


def split_projection(p):
    return jnp.split(p, IN_SPLIT_IDX, axis=-1)


def axial_rope_angles(n_tokens):
    rows_n = n_tokens // GRID_W
    row = jnp.repeat(jnp.arange(rows_n, dtype=jnp.float32), GRID_W)
    col = jnp.tile(jnp.arange(GRID_W, dtype=jnp.float32), rows_n)
    axis_dim = MLA_ROPE // 2
    inv_freq = ROPE_BASE ** (-jnp.arange(0, axis_dim, 2, dtype=jnp.float32) / axis_dim)
    return row[:, None] * inv_freq, col[:, None] * inv_freq


def rotate_half(x, ang):
    x1, x2 = jnp.split(x, 2, axis=-1)
    cos = jnp.cos(ang)[None, :, None, :].astype(x.dtype)
    sin = jnp.sin(ang)[None, :, None, :].astype(x.dtype)
    return jnp.concatenate([x1 * cos - x2 * sin, x1 * sin + x2 * cos], axis=-1)


def apply_axial_rope(x, ang_row, ang_col):
    xr, xc = jnp.split(x, 2, axis=-1)
    return jnp.concatenate([rotate_half(xr, ang_row), rotate_half(xc, ang_col)], axis=-1)


def mla_q(x_q, g_q, w_uq):
    b, l, _ = x_q.shape
    q = (rmsnorm(x_q, g_q) @ w_uq).reshape(b, l, MLA_HEADS, MLA_NOPE + MLA_ROPE)
    return q[..., :MLA_NOPE], q[..., MLA_NOPE:]


def mla_kv(x_kv, x_kr, g_kv, w_ukv):
    b, l, _ = x_kv.shape
    kv = (rmsnorm(x_kv, g_kv) @ w_ukv).reshape(b, l, MLA_HEADS, MLA_NOPE + MLA_V)
    return kv[..., :MLA_NOPE], x_kr[:, :, None, :], kv[..., MLA_NOPE:]


def join_heads(nope, rope):
    rope = jnp.broadcast_to(rope, nope.shape[:-1] + (MLA_ROPE,))
    return jnp.concatenate([nope, rope], axis=-1)


def softmax_attend(q, k, v):
    s = jnp.einsum('bqhd,bkhd->bhqk', q, k).astype(jnp.float32) * MLA_SCALE
    p = jax.nn.softmax(s, axis=-1).astype(v.dtype)
    return jnp.einsum('bhqk,bkhd->bqhd', p, v)


def blocked_attention(q, k, v):
    b, l, h, d = q.shape
    nb = l // Q_BLOCK
    qb = q.reshape(b, nb, Q_BLOCK, h, d).transpose(1, 0, 2, 3, 4)
    o = lax.map(lambda blk: softmax_attend(blk, k, v), qb)
    return o.transpose(1, 0, 2, 3, 4).reshape(b, l, h * v.shape[-1])


def chunk_token_mlp(x_cm, g_norm, w_s, b_s):
    b, l, _ = x_cm.shape
    u, v = jnp.split(jax.nn.gelu(x_cm), 2, axis=-1)
    v = layernorm(v, g_norm).reshape(b, l // CM_CHUNK, CM_CHUNK, CM_GROUPS, CM_GROUP_DIM)
    v = jnp.einsum('gts,bcsgd->bctgd', w_s, v) + b_s.T[:, :, None]
    return u * v.reshape(b, l, CM_WIDTH)


def depthwise_conv(x, w, bias):
    ch = x.shape[-1]
    y = lax.conv_general_dilated(x, w[:, None, :], window_strides=(1,),
                                 padding=[(SSD_CONV // 2, SSD_CONV // 2)],
                                 dimension_numbers=('NWC', 'WIO', 'NWC'),
                                 feature_group_count=ch)
    return y + bias


def segsum(a):
    t = a.shape[-1]
    x = jnp.broadcast_to(a[..., None], a.shape + (t,))
    x = jnp.where(jnp.tril(jnp.ones((t, t), dtype=bool), -1), x, 0.0)
    xs = jnp.cumsum(x, axis=-2)
    return jnp.where(jnp.tril(jnp.ones((t, t), dtype=bool), 0), xs, -jnp.inf)


def ssd_chunked_scan(x, a, bm, cm, init_state, want_y):
    b, l, h, p = x.shape
    nc = l // SSD_CHUNK
    x = x.reshape(b, nc, SSD_CHUNK, h, p)
    bm = bm.reshape(b, nc, SSD_CHUNK, h, -1)
    cm = cm.reshape(b, nc, SSD_CHUNK, h, -1)
    a = a.reshape(b, nc, SSD_CHUNK, h).transpose(0, 3, 1, 2)
    a_cum = jnp.cumsum(a, axis=-1)
    decay_to_end = jnp.exp(a_cum[..., -1:] - a_cum).transpose(0, 2, 3, 1)[..., None]
    chunk_states = jnp.einsum('bclhn,bclhp->bchpn', bm * decay_to_end, x)
    states = jnp.concatenate([init_state[:, None], chunk_states], axis=1)
    chunk_decay = jnp.exp(segsum(jnp.pad(a_cum[..., -1], ((0, 0), (0, 0), (1, 0)))))
    states = jnp.einsum('bhzc,bchpn->bzhpn', chunk_decay, states)
    final_state = states[:, -1]
    if not want_y:
        return None, final_state
    scores = jnp.einsum('bclhn,bcshn->bhcls', cm, bm) * jnp.exp(segsum(a))
    y_diag = jnp.einsum('bhcls,bcshp->bclhp', scores, x)
    decay_from_start = jnp.exp(a_cum).transpose(0, 2, 3, 1)[..., None]
    y_off = jnp.einsum('bclhn,bchpn->bclhp', cm * decay_from_start, states[:, :-1])
    return (y_diag + y_off).reshape(b, l, h, p), final_state


def flip_seq(t, rev):
    return jnp.flip(t, axis=1) if rev else t


def ssd_mixer(z, xbc, dt_raw, conv_w, conv_b, dt_bias, a_log, d_skip, g_norm, init_states, want_y):
    b, l, _ = xbc.shape
    f32 = jnp.float32
    xbc = jax.nn.silu(depthwise_conv(xbc, conv_w, conv_b)).astype(f32)
    xs, bm, cm = jnp.split(xbc, [SSD_WIDTH, SSD_WIDTH + SSD_GROUPS * SSD_STATE], axis=-1)
    xs = xs.reshape(b, l, SSD_HEADS, SSD_HEAD_DIM)
    rep = SSD_HEADS // SSD_GROUPS
    bm = jnp.repeat(bm.reshape(b, l, SSD_GROUPS, SSD_STATE), rep, axis=2)
    cm = jnp.repeat(cm.reshape(b, l, SSD_GROUPS, SSD_STATE), rep, axis=2)
    dt = jax.nn.softplus(dt_raw.astype(f32).reshape(b, l, 2, SSD_HEADS) + dt_bias.astype(f32))
    a = -jnp.exp(a_log.astype(f32))
    d = d_skip.astype(f32)
    ys, finals = [], []
    for direction in range(2):
        rev = direction == 1
        dt_d = flip_seq(dt[:, :, direction], rev)
        x_d = flip_seq(xs, rev)
        y_d, fin = ssd_chunked_scan(x_d * dt_d[..., None], dt_d * a[direction],
                                    flip_seq(bm, rev), flip_seq(cm, rev),
                                    init_states[direction].astype(f32), want_y)
        finals.append(fin)
        if want_y:
            ys.append(flip_seq(y_d, rev) + d[direction][:, None] * xs)
    final_states = jnp.stack(finals)
    if not want_y:
        return None, final_states
    y = (ys[0] + ys[1]).reshape(b, l, SSD_WIDTH).astype(z.dtype)
    return rmsnorm(y * jax.nn.silu(z), g_norm), final_states


def sq_relu_mlp(h, w1, w2):
    return jnp.square(jax.nn.relu(h @ w1)) @ w2


def setup_inputs(seed: int = 0) -> dict:
    key = jax.random.key(seed)
    ks = jax.random.split(key, 32)
    f32 = jnp.float32

    def nrm(k, shape, scale):
        return jax.random.normal(k, shape, f32) * scale

    def gain(k, shape):
        return 1.0 + 0.05 * jax.random.normal(k, shape, f32)

    dt0 = jnp.exp(jax.random.uniform(ks[20], (DEPTH, 2, SSD_HEADS), f32, math.log(1e-3), math.log(1e-1)))
    return {
        'x': nrm(ks[0], (BATCH, SEQ, D_MODEL), 1.0),
        'c': nrm(ks[1], (BATCH, D_MODEL), 1.0),
        'ctx': nrm(ks[2], (BATCH, CTX_LEN, D_MODEL), 1.0),
        'c_ctx': nrm(ks[3], (D_MODEL,), 1.0),
        'w_ada': nrm(ks[4], (DEPTH, D_MODEL, 6 * D_MODEL), 0.5 * D_MODEL ** -0.5),
        'b_ada': nrm(ks[5], (DEPTH, 6 * D_MODEL), 0.02),
        'g_pre_mix': gain(ks[6], (DEPTH, D_MODEL)),
        'g_post_mix': gain(ks[7], (DEPTH, D_MODEL)),
        'g_pre_ff': gain(ks[8], (DEPTH, D_MODEL)),
        'g_post_ff': gain(ks[9], (DEPTH, D_MODEL)),
        'w_in': nrm(ks[10], (DEPTH, D_MODEL, IN_WIDTH), D_MODEL ** -0.5),
        'g_q': gain(ks[11], (DEPTH, MLA_Q_LORA)),
        'w_uq': nrm(ks[12], (DEPTH, MLA_Q_LORA, MLA_HEADS * (MLA_NOPE + MLA_ROPE)), MLA_Q_LORA ** -0.5),
        'g_kv': gain(ks[13], (DEPTH, MLA_KV_LORA)),
        'w_ukv': nrm(ks[14], (DEPTH, MLA_KV_LORA, MLA_HEADS * (MLA_NOPE + MLA_V)), MLA_KV_LORA ** -0.5),
        'cm_norm_g': gain(ks[15], (DEPTH, CM_WIDTH)),
        'cm_w_s': nrm(ks[16], (DEPTH, CM_GROUPS, CM_CHUNK, CM_CHUNK), CM_CHUNK ** -0.5),
        'cm_b_s': 1.0 + nrm(ks[17], (DEPTH, CM_GROUPS, CM_CHUNK), 0.02),
        'ssd_conv_w': nrm(ks[18], (DEPTH, SSD_CONV, SSD_CONV_DIM), SSD_CONV ** -0.5),
        'ssd_conv_b': nrm(ks[19], (DEPTH, SSD_CONV_DIM), 0.02),
        'ssd_dt_bias': dt0 + jnp.log(-jnp.expm1(-dt0)),
        'ssd_a_log': jnp.log(jax.random.uniform(ks[21], (DEPTH, 2, SSD_HEADS), f32, 1.0, 16.0)),
        'ssd_d': gain(ks[22], (DEPTH, 2, SSD_HEADS)),
        'ssd_norm_g': gain(ks[23], (DEPTH, SSD_WIDTH)),
        'w_out': nrm(ks[24], (DEPTH, MIX_WIDTH, D_MODEL), MIX_WIDTH ** -0.5),
        'w_ff1': nrm(ks[25], (DEPTH, D_MODEL, D_FF), D_MODEL ** -0.5),
        'w_ff2': nrm(ks[26], (DEPTH, D_FF, D_MODEL), D_FF ** -0.5),
    }


def reference(x, c, ctx, c_ctx, w_ada, b_ada, g_pre_mix, g_post_mix, g_pre_ff, g_post_ff,
              w_in, g_q, w_uq, g_kv, w_ukv, cm_norm_g, cm_w_s, cm_b_s,
              ssd_conv_w, ssd_conv_b, ssd_dt_bias, ssd_a_log, ssd_d, ssd_norm_g,
              w_out, w_ff1, w_ff2):
    b, n_lat, _ = x.shape
    n_ctx = ctx.shape[1]
    ang_row, ang_col = axial_rope_angles(n_lat)
    xc = ctx
    for layer in range(DEPTH):
        last = layer == DEPTH - 1
        mod = (jax.nn.silu(c) @ w_ada[layer] + b_ada[layer])[:, None, :]
        mod_c = (jax.nn.silu(c_ctx) @ w_ada[layer] + b_ada[layer])[None, None, :]
        shift1, scale1, gate1, shift2, scale2, gate2 = jnp.split(mod, 6, axis=-1)
        cshift1, cscale1, cgate1, cshift2, cscale2, cgate2 = jnp.split(mod_c, 6, axis=-1)
        ssd_params = (ssd_conv_w[layer], ssd_conv_b[layer], ssd_dt_bias[layer],
                      ssd_a_log[layer], ssd_d[layer], ssd_norm_g[layer])

        hc = modulate(rmsnorm(xc, g_pre_mix[layer]), cshift1, cscale1)
        cq_lo, ckv_lo, ckr, ccm, cz, cxbc, cdt = split_projection(hc @ w_in[layer])
        ck_nope, ck_rope, cv = mla_kv(ckv_lo, ckr, g_kv[layer], w_ukv[layer])
        ck = join_heads(ck_nope, ck_rope)
        zero_states = jnp.zeros((2, b, SSD_HEADS, SSD_HEAD_DIM, SSD_STATE), jnp.float32)
        cy_ssd, ctx_states = ssd_mixer(cz, cxbc, cdt, *ssd_params, zero_states, not last)
        if not last:
            cq_nope, cq_rope = mla_q(cq_lo, g_q[layer], w_uq[layer])
            cy_attn = softmax_attend(join_heads(cq_nope, cq_rope), ck, cv).reshape(b, n_ctx, MLA_WIDTH)
            cy_cm = chunk_token_mlp(ccm, cm_norm_g[layer], cm_w_s[layer], cm_b_s[layer])
            cy = jnp.concatenate([cy_attn, cy_cm, cy_ssd], axis=-1) @ w_out[layer]
            xc = xc + cgate1 * rmsnorm(cy, g_post_mix[layer])
            hc2 = modulate(rmsnorm(xc, g_pre_ff[layer]), cshift2, cscale2)
            xc = xc + cgate2 * rmsnorm(sq_relu_mlp(hc2, w_ff1[layer], w_ff2[layer]), g_post_ff[layer])

        h = modulate(rmsnorm(x, g_pre_mix[layer]), shift1, scale1)
        q_lo, kv_lo, kr, xcm, z, xbc, dt = split_projection(h @ w_in[layer])
        q_nope, q_rope = mla_q(q_lo, g_q[layer], w_uq[layer])
        k_nope, k_rope, v = mla_kv(kv_lo, kr, g_kv[layer], w_ukv[layer])
        q = join_heads(q_nope, apply_axial_rope(q_rope, ang_row, ang_col))
        k = join_heads(k_nope, apply_axial_rope(k_rope, ang_row, ang_col))
        y_attn = blocked_attention(q, jnp.concatenate([k, ck], axis=1), jnp.concatenate([v, cv], axis=1))
        y_cm = chunk_token_mlp(xcm, cm_norm_g[layer], cm_w_s[layer], cm_b_s[layer])
        y_ssd, _ = ssd_mixer(z, xbc, dt, *ssd_params, ctx_states, True)
        y = jnp.concatenate([y_attn, y_cm, y_ssd], axis=-1) @ w_out[layer]
        x = x + gate1 * rmsnorm(y, g_post_mix[layer])
        h2 = modulate(rmsnorm(x, g_pre_ff[layer]), shift2, scale2)
        x = x + gate2 * rmsnorm(sq_relu_mlp(h2, w_ff1[layer], w_ff2[layer]), g_post_ff[layer])
    return x
```

```python
import functools
import math

import numpy as np
import jax
import jax.numpy as jnp
from jax import lax
from jax.experimental import pallas as pl
from jax.experimental.pallas import tpu as pltpu

F32 = jnp.float32
BF16 = jnp.bfloat16

EPS = 1e-6
GRID_W = 64
ROPE_BASE = 10000.0

HEADS = 4
Q_LORA = 256
KV_LORA = 128
NOPE = 128
ROPE = 64
VDIM = 128
QK_PAD = 256
ATTN_SCALE = (NOPE + ROPE) ** -0.5

CM_GROUPS = 4
CM_WIDTH = 256
CM_GROUP_DIM = CM_WIDTH // CM_GROUPS
CHUNK = 128

SSD_WIDTH = 256
SSD_HEADS = 4
SSD_GROUPS = 2
SSD_STATE = 128
SSD_CONV_DIM = SSD_WIDTH + 2 * SSD_GROUPS * SSD_STATE

LANE = 128
SUBLANE = 8
MOD_ROWS = 16

P_Q, P_KV, P_KR, P_CM, P_Z, P_XBC, P_DT, P_END = 0, 256, 384, 512, 1024, 1280, 2048, 2176

VMEM_LIMIT = 56 * 1024 * 1024


def _rms(x, g):
    ms = jnp.mean(x * x, axis=-1, keepdims=True)
    return x * lax.rsqrt(ms + EPS) * g


def _sigmoid(x):
    return 1.0 / (1.0 + jnp.exp(-x))


def _silu(x):
    return x * _sigmoid(x)


def _dot(a, b):
    return jnp.dot(a, b, preferred_element_type=F32)


def _dot_nt(a, b):
    return lax.dot_general(a, b, (((1,), (1,)), ((), ())), preferred_element_type=F32)


def _full_spec(shape):
    zeros = (0,) * len(shape)
    return pl.BlockSpec(shape, lambda *_: zeros)


def _mod_kernel(c_ref, w_ref, b_ref, o_ref):
    s = _silu(c_ref[...]).astype(BF16)
    o_ref[...] = _dot(s, w_ref[...].astype(BF16)) + b_ref[...]


def _mod_call(cc, w_ada, b_ada):
    depth, d, n = w_ada.shape
    tn = 512
    return pl.pallas_call(
        _mod_kernel,
        out_shape=jax.ShapeDtypeStruct((depth, MOD_ROWS, n), F32),
        grid=(depth, n // tn),
        in_specs=[
            pl.BlockSpec((MOD_ROWS, d), lambda l, j: (0, 0)),
            pl.BlockSpec((None, d, tn), lambda l, j: (l, 0, j)),
            pl.BlockSpec((None, 1, tn), lambda l, j: (l, 0, j)),
        ],
        out_specs=pl.BlockSpec((None, MOD_ROWS, tn), lambda l, j: (l, 0, j)),
        compiler_params=pltpu.CompilerParams(dimension_semantics=("arbitrary", "arbitrary")),
        name="adaln_rows",
    )(cc, w_ada, b_ada.reshape(depth, 1, n))


def _pre_kernel(x_ref, mod_ref, gpre_ref, win_ref, gq_ref, wuq_ref, gkv_ref, wukv_ref,
                cos_ref, sin_ref, cmg_ref, cmw_ref, cmb_ref,
                q_ref, k_ref, v_ref, ycm_ref, z_ref, xbc_ref, dt_ref):
    d = x_ref.shape[-1]
    tm = x_ref.shape[0]
    shift = mod_ref[:, 0:d]
    scale = mod_ref[:, d:2 * d]
    h = (_rms(x_ref[...], gpre_ref[...]) * (1.0 + scale) + shift).astype(BF16)

    def proj(lo, hi):
        return _dot(h, win_ref[:, lo:hi])

    cos = cos_ref[...]
    sin = sin_ref[...]

    def rotate(rs):
        return rs * cos + pltpu.roll(rs, ROPE, 1) * sin

    qn = _rms(proj(P_Q, P_KV), gq_ref[...]).astype(BF16)
    qf = _dot(qn, wuq_ref[...])
    for hh in range(HEADS):
        lo = hh * QK_PAD
        q_ref[:, lo:lo + NOPE] = qf[:, lo:lo + NOPE].astype(BF16)
        q_ref[:, lo + NOPE:lo + QK_PAD] = rotate(qf[:, lo + NOPE:lo + QK_PAD]).astype(BF16)

    kvn = _rms(proj(P_KV, P_KR), gkv_ref[...]).astype(BF16)
    kvf = _dot(kvn, wukv_ref[...])
    krot = rotate(proj(P_KR, P_CM)).astype(BF16)
    for hh in range(HEADS):
        lo = hh * QK_PAD
        k_ref[:, lo:lo + NOPE] = kvf[:, hh * (NOPE + VDIM):hh * (NOPE + VDIM) + NOPE].astype(BF16)
        k_ref[:, lo + NOPE:lo + QK_PAD] = krot
        v_ref[:, hh * VDIM:(hh + 1) * VDIM] = kvf[:, hh * (NOPE + VDIM) + NOPE:(hh + 1) * (NOPE + VDIM)].astype(BF16)

    xcm = proj(P_CM, P_Z)
    gl = xcm * (0.5 * (1.0 + jnp.tanh(math.sqrt(2.0 / math.pi) * (xcm + 0.044715 * (xcm * xcm * xcm)))))
    u = gl[:, :CM_WIDTH]
    vv = gl[:, CM_WIDTH:]
    mu = jnp.mean(vv, axis=-1, keepdims=True)
    vc = vv - mu
    var = jnp.mean(vc * vc, axis=-1, keepdims=True)
    vn = vc * lax.rsqrt(var + EPS) * cmg_ref[...]
    lane_group = lax.broadcasted_iota(jnp.int32, (CHUNK, CM_WIDTH), 1) // CM_GROUP_DIM
    for c in range(tm // CHUNK):
        rows = slice(c * CHUNK, (c + 1) * CHUNK)
        vchunk = vn[rows, :]
        vblk = jnp.concatenate([jnp.where(lane_group == g, vchunk, 0.0).astype(BF16) for g in range(CM_GROUPS)],
                               axis=0)
        mixed = _dot(cmw_ref[...], vblk) + cmb_ref[...]
        ycm_ref[rows, :] = (u[rows, :] * mixed).astype(BF16)

    z_ref[...] = proj(P_Z, P_XBC)
    xbc_ref[...] = proj(P_XBC, P_DT)
    dt_ref[...] = proj(P_DT, P_END)


def _pre_call(x, mod, mod_row, lw, cos, sin, tm):
    b, l, d = x.shape
    grid = (b, l // tm)
    row_spec = lambda w: pl.BlockSpec((None, tm, w), lambda bi, i: (bi, i, 0))
    if mod_row is None:
        mod_spec = pl.BlockSpec((None, 1, mod.shape[-1]), lambda bi, i: (bi, 0, 0))
    else:
        mod_spec = pl.BlockSpec((None, 1, mod.shape[-1]), lambda bi, i: (mod_row, 0, 0))
    tab_spec = pl.BlockSpec((tm, LANE), lambda bi, i: (i, 0))
    weights = [lw["g_pre_mix"], lw["w_in"], lw["g_q"], lw["w_uq"], lw["g_kv"], lw["w_ukv"]]
    cm_weights = [lw["cm_norm_g"], lw["cm_w"], lw["cm_b"]]
    out_widths = [(HEADS * QK_PAD, BF16), (HEADS * QK_PAD, BF16), (HEADS * VDIM, BF16), (CM_WIDTH, BF16),
                  (SSD_WIDTH, F32), (SSD_CONV_DIM, F32), (LANE, F32)]
    return pl.pallas_call(
        _pre_kernel,
        out_shape=[jax.ShapeDtypeStruct((b, l, w), dt) for w, dt in out_widths],
        grid=grid,
        in_specs=[row_spec(d), mod_spec] + [_full_spec(w.shape) for w in weights] + [tab_spec, tab_spec]
        + [_full_spec(w.shape) for w in cm_weights],
        out_specs=[row_spec(w) for w, _ in out_widths],
        compiler_params=pltpu.CompilerParams(dimension_semantics=("arbitrary", "arbitrary"),
                                             vmem_limit_bytes=VMEM_LIMIT),
        name="pre_mix",
    )(x, mod, *weights, cos, sin, *cm_weights)


def _attn_kernel(q_ref, *refs, tk):
    o_ref = refs[-1]
    kv_refs = refs[:-1]
    tq = q_ref.shape[0]
    q = q_ref[...]
    c = ATTN_SCALE * math.log2(math.e)
    m = jnp.full((tq, 1), -1e30, F32)
    l = jnp.zeros((tq, 1), F32)
    acc = jnp.zeros((tq, VDIM), F32)
    for seg in range(len(kv_refs) // 2):
        k_ref, v_ref = kv_refs[2 * seg], kv_refs[2 * seg + 1]
        n = k_ref.shape[0]
        step = min(tk, n)
        for start in range(0, n, step):
            s = _dot_nt(q, k_ref[start:start + step, :])
            m_new = jnp.maximum(m, jnp.max(s, axis=1, keepdims=True))
            p = jnp.exp2((s - m_new) * c)
            alpha = jnp.exp2((m - m_new) * c)
            l = alpha * l + jnp.sum(p, axis=1, keepdims=True)
            acc = alpha * acc + _dot(p.astype(BF16), v_ref[start:start + step, :])
            m = m_new
    o_ref[...] = (acc * (1.0 / l)).astype(BF16)


def _attn_call(q, kvs, tq, tk):
    b, lq, _ = q.shape
    in_specs = [pl.BlockSpec((None, tq, QK_PAD), lambda bi, hi, i: (bi, i, hi))]
    args = [q]
    for k, v in kvs:
        lk = k.shape[1]
        in_specs.append(pl.BlockSpec((None, lk, QK_PAD), lambda bi, hi, i: (bi, 0, hi)))
        in_specs.append(pl.BlockSpec((None, lk, VDIM), lambda bi, hi, i: (bi, 0, hi)))
        args += [k, v]
    return pl.pallas_call(
        functools.partial(_attn_kernel, tk=tk),
        out_shape=jax.ShapeDtypeStruct((b, lq, HEADS * VDIM), BF16),
        grid=(b, HEADS, lq // tq),
        in_specs=in_specs,
        out_specs=pl.BlockSpec((None, tq, VDIM), lambda bi, hi, i: (bi, i, hi)),
        compiler_params=pltpu.CompilerParams(dimension_semantics=("arbitrary", "arbitrary", "arbitrary"),
                                             vmem_limit_bytes=VMEM_LIMIT),
        name="mla_attention",
    )(*args)


def _ssd_kernel(xm_f, xp_f, xn_f, dt_f, xm_b, xp_b, xn_b, dt_b,
                convw_ref, convb_ref, dtbias_ref, alog_ref, drep_ref, init_ref,
                yf_ref, yb_ref, fin_ref, st_ref, *, nc):
    j = pl.program_id(1)

    @pl.when(j == 0)
    def _():
        st_ref[...] = init_ref[...]

    a_row = -jnp.exp(alog_ref[...])
    row_id = lax.broadcasted_iota(jnp.int32, (CHUNK, SSD_CONV_DIM), 0)
    ri = lax.broadcasted_iota(jnp.int32, (CHUNK, CHUNK), 0)
    ci = lax.broadcasted_iota(jnp.int32, (CHUNK, CHUNK), 1)
    low_half = ci < (LANE // 2)
    w = convw_ref[...]
    streams = ((xm_f, xp_f, xn_f, dt_f, yf_ref), (xm_b, xp_b, xn_b, dt_b, yb_ref))
    for d, (xm, xp, xn, dtr, y_ref) in enumerate(streams):
        chunk = j if d == 0 else nc - 1 - j
        x = xm[...]
        prev_row = jnp.where(chunk > 0, xp[SUBLANE - 1:SUBLANE, :], 0.0)
        next_row = jnp.where(chunk < nc - 1, xn[0:1, :], 0.0)
        x_prev = jnp.where(row_id == 0, prev_row, pltpu.roll(x, 1, 0))
        x_next = jnp.where(row_id == CHUNK - 1, next_row, pltpu.roll(x, CHUNK - 1, 0))
        xa = _silu(w[0:1, :] * x_prev + w[1:2, :] * x + w[2:3, :] * x_next + convb_ref[...])
        xs = xa[:, :SSD_WIDTH]
        bm = xa[:, SSD_WIDTH:SSD_WIDTH + SSD_GROUPS * SSD_STATE]
        cm = xa[:, SSD_WIDTH + SSD_GROUPS * SSD_STATE:]

        dt_in = dtr[...] + dtbias_ref[...]
        dtv = jnp.maximum(dt_in, 0.0) + jnp.log1p(jnp.exp(-jnp.abs(dt_in)))
        a = dtv * a_row
        a_hi = a.astype(BF16)
        r1 = a - a_hi.astype(F32)
        a_mid = r1.astype(BF16)
        a_lo = (r1 - a_mid.astype(F32)).astype(BF16)
        causal = (ri >= ci) if d == 0 else (ri <= ci)
        tri = jnp.where(causal, 1.0, 0.0).astype(BF16)
        cum = _dot(tri, a_hi) + _dot(tri, a_mid) + _dot(tri, a_lo)
        cum_t = cum.T
        tot = cum[CHUNK - 1:CHUNK, :] if d == 0 else cum[0:1, :]

        for g in range(SSD_GROUPS):
            i0 = d * SSD_HEADS + 2 * g
            i1 = i0 + 1
            gl = slice(g * LANE, (g + 1) * LANE)
            colx = jnp.where(low_half, cum[:, i0:i0 + 1], cum[:, i1:i1 + 1])
            dtx = jnp.where(low_half, dtv[:, i0:i0 + 1], dtv[:, i1:i1 + 1])
            totx = jnp.where(low_half[0:1, :], tot[:, i0:i0 + 1], tot[:, i1:i1 + 1])
            xs_g = xs[:, gl]
            xdt = xs_g * dtx
            xdt_b = xdt.astype(BF16)
            xdec_b = (xdt * jnp.exp(totx - colx)).astype(BF16)
            cm_b = cm[:, gl].astype(BF16)
            bm_g = bm[:, gl]
            gmat = _dot_nt(cm_b, bm_g.astype(BF16))
            ys = []
            for ii in (i0, i1):
                diff = cum[:, ii:ii + 1] - cum_t[ii:ii + 1, :]
                lmat = jnp.exp(jnp.where(causal, diff, -1e30))
                ys.append(_dot((gmat * lmat).astype(BF16), xdt_b))
            y_diag = jnp.where(low_half, ys[0], ys[1])
            st = st_ref[d, g]
            y_off = _dot(cm_b, st.astype(BF16)) * jnp.exp(colx)
            y = y_diag + y_off
            if d == 0:
                y = y + (drep_ref[0:1, gl] + drep_ref[1:2, gl]) * xs_g
            y_ref[:, gl] = y
            st_ref[d, g] = jnp.exp(totx) * st + _dot(bm_g.T.astype(BF16), xdec_b)

    @pl.when(j == nc - 1)
    def _():
        fin_ref[...] = st_ref[...]


def _ssd_call(xbc, dt, lw, init):
    b, l, _ = xbc.shape
    nc = l // CHUNK
    per8 = CHUNK // SUBLANE
    last8 = l // SUBLANE - 1
    fwd = lambda bi, j: (bi, j, 0)
    bwd = lambda bi, j: (bi, nc - 1 - j, 0)
    fwd_prev = lambda bi, j: (bi, jnp.maximum(j * per8 - 1, 0), 0)
    fwd_next = lambda bi, j: (bi, jnp.minimum((j + 1) * per8, last8), 0)
    bwd_prev = lambda bi, j: (bi, jnp.maximum((nc - 1 - j) * per8 - 1, 0), 0)
    bwd_next = lambda bi, j: (bi, jnp.minimum((nc - j) * per8, last8), 0)
    main = lambda f: pl.BlockSpec((None, CHUNK, SSD_CONV_DIM), f)
    halo = lambda f: pl.BlockSpec((None, SUBLANE, SSD_CONV_DIM), f)
    dts = lambda f: pl.BlockSpec((None, CHUNK, LANE), f)
    small = [lw["ssd_conv_w"], lw["ssd_conv_b"], lw["ssd_dt_bias"], lw["ssd_a_log"], lw["ssd_d"]]
    st_shape = (2, SSD_GROUPS, SSD_STATE, LANE)
    st_spec = pl.BlockSpec((None,) + st_shape, lambda bi, j: (bi, 0, 0, 0, 0))
    y_shape = jax.ShapeDtypeStruct((b, l, SSD_WIDTH), F32)
    return pl.pallas_call(
        functools.partial(_ssd_kernel, nc=nc),
        out_shape=[y_shape, y_shape, jax.ShapeDtypeStruct((b,) + st_shape, F32)],
        grid=(b, nc),
        in_specs=[main(fwd), halo(fwd_prev), halo(fwd_next), dts(fwd),
                  main(bwd), halo(bwd_prev), halo(bwd_next), dts(bwd)]
        + [_full_spec(w.shape) for w in small] + [st_spec],
        out_specs=[pl.BlockSpec((None, CHUNK, SSD_WIDTH), fwd), pl.BlockSpec((None, CHUNK, SSD_WIDTH), bwd), st_spec],
        scratch_shapes=[pltpu.VMEM(st_shape, F32)],
        compiler_params=pltpu.CompilerParams(dimension_semantics=("arbitrary", "arbitrary"),
                                             vmem_limit_bytes=VMEM_LIMIT),
        name="ssd_scan",
    )(xbc, xbc, xbc, dt, xbc, xbc, xbc, dt, *small, init)


def _post_kernel(x_ref, ya_ref, ycm_ref, yf_ref, yb_ref, z_ref, mod_ref,
                 gssd_ref, gpm_ref, gpf_ref, gpo_ref, wout_ref, w1_ref, w2_ref, o_ref, *, tf):
    d = x_ref.shape[-1]
    f = w1_ref.shape[-1]
    gate1 = mod_ref[:, 2 * d:3 * d]
    shift2 = mod_ref[:, 3 * d:4 * d]
    scale2 = mod_ref[:, 4 * d:5 * d]
    gate2 = mod_ref[:, 5 * d:6 * d]
    y_ssd = _rms((yf_ref[...] + yb_ref[...]) * _silu(z_ref[...]), gssd_ref[...]).astype(BF16)
    wa = HEADS * VDIM
    y = (_dot(ya_ref[...], wout_ref[0:wa, :]) + _dot(ycm_ref[...], wout_ref[wa:wa + CM_WIDTH, :])
         + _dot(y_ssd, wout_ref[wa + CM_WIDTH:, :]))
    x1 = x_ref[...] + gate1 * _rms(y, gpm_ref[...])
    h2 = (_rms(x1, gpf_ref[...]) * (1.0 + scale2) + shift2).astype(BF16)
    acc = jnp.zeros(x1.shape, F32)
    for c in range(f // tf):
        hc = jnp.maximum(_dot(h2, w1_ref[:, c * tf:(c + 1) * tf]), 0.0)
        acc = acc + _dot((hc * hc).astype(BF16), w2_ref[c * tf:(c + 1) * tf, :])
    o_ref[...] = x1 + gate2 * _rms(acc, gpo_ref[...])


def _post_call(x, ya, ycm, yf, yb, z, mod, mod_row, lw, tm):
    b, l, d = x.shape
    row_spec = lambda w: pl.BlockSpec((None, tm, w), lambda bi, i: (bi, i, 0))
    if mod_row is None:
        mod_spec = pl.BlockSpec((None, 1, mod.shape[-1]), lambda bi, i: (bi, 0, 0))
    else:
        mod_spec = pl.BlockSpec((None, 1, mod.shape[-1]), lambda bi, i: (mod_row, 0, 0))
    weights = [lw["ssd_norm_g"], lw["g_post_mix"], lw["g_pre_ff"], lw["g_post_ff"], lw["w_out"], lw["w_ff1"], lw["w_ff2"]]
    single = lambda w: pl.BlockSpec(w.shape, lambda *_: (0,) * w.ndim, pipeline_mode=pl.Buffered(1))
    return pl.pallas_call(
        functools.partial(_post_kernel, tf=1024),
        out_shape=jax.ShapeDtypeStruct((b, l, d), F32),
        grid=(b, l // tm),
        in_specs=[row_spec(d), row_spec(HEADS * VDIM), row_spec(CM_WIDTH), row_spec(SSD_WIDTH), row_spec(SSD_WIDTH),
                  row_spec(SSD_WIDTH), mod_spec] + [single(w) for w in weights],
        out_specs=row_spec(d),
        compiler_params=pltpu.CompilerParams(dimension_semantics=("arbitrary", "arbitrary"),
                                             vmem_limit_bytes=VMEM_LIMIT),
        name="post_mix_ffn",
    )(x, ya, ycm, yf, yb, z, mod, *weights)


def _rope_tables(n_tokens):
    rows_n = n_tokens // GRID_W
    row = jnp.repeat(jnp.arange(rows_n, dtype=F32), GRID_W)
    col = jnp.tile(jnp.arange(GRID_W, dtype=F32), rows_n)
    axis_dim = ROPE // 2
    inv_freq = ROPE_BASE ** (-jnp.arange(0, axis_dim, 2, dtype=F32) / axis_dim)
    ang_r = row[:, None] * inv_freq
    ang_c = col[:, None] * inv_freq
    zeros = jnp.zeros((n_tokens, LANE - ROPE), F32)
    cos = jnp.concatenate([jnp.cos(ang_r), jnp.cos(ang_r), jnp.cos(ang_c), jnp.cos(ang_c), zeros], axis=-1)
    sin = jnp.concatenate([-jnp.sin(ang_r), jnp.sin(ang_r), -jnp.sin(ang_c), jnp.sin(ang_c), zeros], axis=-1)
    return cos, sin


def _rotate_half_partner():
    q = ROPE // 4
    return np.concatenate([np.arange(q, 2 * q), np.arange(0, q), np.arange(3 * q, 4 * q), np.arange(2 * q, 3 * q)])


def _pack_weights(p):
    depth = p["w_in"].shape[0]
    partner = _rotate_half_partner()
    w_in = p["w_in"]
    kr = w_in[..., 384:448]
    pad = jnp.zeros(w_in.shape[:-1] + (LANE - 2 * SSD_HEADS,), w_in.dtype)
    w_in_p = jnp.concatenate([w_in[..., :384], kr, kr[..., partner], w_in[..., 448:1984], w_in[..., 1984:], pad], axis=-1)
    w_uq = p["w_uq"].reshape(depth, Q_LORA, HEADS, NOPE + ROPE)
    rope_cols = w_uq[..., NOPE:]
    w_uq_p = jnp.concatenate([w_uq[..., :NOPE], rope_cols, rope_cols[..., partner]], axis=-1)
    row = lambda a: a.reshape(depth, 1, -1)
    pad_lane = lambda a: jnp.pad(a.reshape(depth, 1, -1), ((0, 0), (0, 0), (0, LANE - 2 * SSD_HEADS)))
    return {
        "g_pre_mix": row(p["g_pre_mix"]), "g_post_mix": row(p["g_post_mix"]),
        "g_pre_ff": row(p["g_pre_ff"]), "g_post_ff": row(p["g_post_ff"]),
        "w_in": w_in_p.astype(BF16),
        "g_q": row(p["g_q"]), "w_uq": w_uq_p.reshape(depth, Q_LORA, HEADS * QK_PAD).astype(BF16),
        "g_kv": row(p["g_kv"]), "w_ukv": p["w_ukv"].astype(BF16),
        "cm_norm_g": row(p["cm_norm_g"]),
        "cm_w": jnp.transpose(p["cm_w_s"], (0, 2, 1, 3)).reshape(depth, CHUNK, CM_GROUPS * CHUNK).astype(BF16),
        "cm_b": jnp.repeat(jnp.transpose(p["cm_b_s"], (0, 2, 1)), CM_GROUP_DIM, axis=-1),
        "ssd_conv_w": p["ssd_conv_w"], "ssd_conv_b": row(p["ssd_conv_b"]),
        "ssd_dt_bias": pad_lane(p["ssd_dt_bias"]), "ssd_a_log": pad_lane(p["ssd_a_log"]),
        "ssd_d": jnp.repeat(p["ssd_d"], SSD_WIDTH // SSD_HEADS, axis=-1),
        "ssd_norm_g": row(p["ssd_norm_g"]),
        "w_out": p["w_out"].astype(BF16), "w_ff1": p["w_ff1"].astype(BF16), "w_ff2": p["w_ff2"].astype(BF16),
    }


def _row_tile(l, want):
    return want if l % want == 0 else l


def kernel(x, c, ctx, c_ctx, w_ada, b_ada, g_pre_mix, g_post_mix, g_pre_ff, g_post_ff, w_in, g_q, w_uq, g_kv, w_ukv, cm_norm_g, cm_w_s, cm_b_s, ssd_conv_w, ssd_conv_b, ssd_dt_bias, ssd_a_log, ssd_d, ssd_norm_g, w_out, w_ff1, w_ff2):
    b, n_lat, d = x.shape
    n_ctx = ctx.shape[1]
    depth = w_ada.shape[0]
    assert b + 1 <= MOD_ROWS and n_lat % CHUNK == 0 and n_ctx % CHUNK == 0 and n_lat % GRID_W == 0

    packed = _pack_weights(dict(
        g_pre_mix=g_pre_mix, g_post_mix=g_post_mix, g_pre_ff=g_pre_ff, g_post_ff=g_post_ff, w_in=w_in, g_q=g_q,
        w_uq=w_uq, g_kv=g_kv, w_ukv=w_ukv, cm_norm_g=cm_norm_g, cm_w_s=cm_w_s, cm_b_s=cm_b_s,
        ssd_conv_w=ssd_conv_w, ssd_conv_b=ssd_conv_b, ssd_dt_bias=ssd_dt_bias, ssd_a_log=ssd_a_log, ssd_d=ssd_d,
        ssd_norm_g=ssd_norm_g, w_out=w_out, w_ff1=w_ff1, w_ff2=w_ff2))

    cc = jnp.concatenate([c, c_ctx[None, :], jnp.zeros((MOD_ROWS - b - 1, d), c.dtype)], axis=0)
    mods = _mod_call(cc, w_ada, b_ada)

    cos_l, sin_l = _rope_tables(n_lat)
    cos_c = jnp.concatenate([jnp.ones((n_ctx, ROPE), F32), jnp.zeros((n_ctx, LANE - ROPE), F32)], axis=-1)
    sin_c = jnp.zeros((n_ctx, LANE), F32)

    tm_l = _row_tile(n_lat, 512)
    tm_c = _row_tile(n_ctx, 256)
    tk = 512
    zero_states = jnp.zeros((b, 2, SSD_GROUPS, SSD_STATE, LANE), F32)

    xc = ctx
    for layer in range(depth):
        last = layer == depth - 1
        lw = {name: w[layer] for name, w in packed.items()}
        mod = mods[layer].reshape(MOD_ROWS, 1, 6 * d)

        cq, ck, cv, cycm, cz, cxbc, cdt = _pre_call(xc, mod, b, lw, cos_c, sin_c, tm_c)
        cyf, cyb, ctx_states = _ssd_call(cxbc, cdt, lw, zero_states)
        if not last:
            cya = _attn_call(cq, [(ck, cv)], tm_c, tk)
            xc = _post_call(xc, cya, cycm, cyf, cyb, cz, mod, b, lw, tm_c)

        q, k, v, ycm, z, xbc, dt = _pre_call(x, mod, None, lw, cos_l, sin_l, tm_l)
        yf, yb, _ = _ssd_call(xbc, dt, lw, ctx_states)
        ya = _attn_call(q, [(k, v), (ck, cv)], tm_l, tk)
        x = _post_call(x, ya, ycm, yf, yb, z, mod, None, lw, tm_l)
    return x
```

```python
import functools
import math

import numpy as np
import jax
import jax.numpy as jnp
from jax import lax
from jax.experimental import pallas as pl
from jax.experimental.pallas import tpu as pltpu

F32 = jnp.float32
BF16 = jnp.bfloat16

EPS = 1e-6
GRID_W = 64
ROPE_BASE = 10000.0

HEADS = 4
Q_LORA = 256
KV_LORA = 128
NOPE = 128
ROPE = 64
VDIM = 128
QK_PAD = 256
ATTN_SCALE = (NOPE + ROPE) ** -0.5
QSCALE = ATTN_SCALE * math.log2(math.e)

CM_GROUPS = 4
CM_WIDTH = 256
CM_GROUP_DIM = CM_WIDTH // CM_GROUPS
CHUNK = 128

SSD_WIDTH = 256
SSD_HEADS = 4
SSD_GROUPS = 2
SSD_STATE = 128
SSD_CONV_DIM = SSD_WIDTH + 2 * SSD_GROUPS * SSD_STATE

LANE = 128
SUBLANE = 8
MOD_ROWS = 16
SUB_ROWS = 256

P_Q, P_KV, P_KR, P_CM, P_Z, P_XBC, P_DT, P_END = 0, 256, 384, 512, 1024, 1280, 2048, 2176

VMEM_LIMIT = 56 * 1024 * 1024


def _rms(x, g):
    ms = jnp.mean(x * x, axis=-1, keepdims=True)
    return x * lax.rsqrt(ms + EPS) * g


def _sigmoid(x):
    return 1.0 / (1.0 + jnp.exp(-x))


def _silu(x):
    return x * _sigmoid(x)


def _dot(a, b):
    return jnp.dot(a, b, preferred_element_type=F32)


def _dot_nt(a, b):
    return lax.dot_general(a, b, (((1,), (1,)), ((), ())), preferred_element_type=F32)


def _full_spec(shape):
    zeros = (0,) * len(shape)
    return pl.BlockSpec(shape, lambda *_: zeros)


def _mod_kernel(c_ref, w_ref, b_ref, o_ref):
    s = _silu(c_ref[...]).astype(BF16)
    o_ref[...] = _dot(s, w_ref[...].astype(BF16)) + b_ref[...]


def _mod_call(cc, w_ada, b_ada):
    depth, d, n = w_ada.shape
    tn = 512
    return pl.pallas_call(
        _mod_kernel,
        out_shape=jax.ShapeDtypeStruct((depth, MOD_ROWS, n), F32),
        grid=(depth, n // tn),
        in_specs=[
            pl.BlockSpec((MOD_ROWS, d), lambda l, j: (0, 0)),
            pl.BlockSpec((None, d, tn), lambda l, j: (l, 0, j)),
            pl.BlockSpec((None, 1, tn), lambda l, j: (l, 0, j)),
        ],
        out_specs=pl.BlockSpec((None, MOD_ROWS, tn), lambda l, j: (l, 0, j)),
        compiler_params=pltpu.CompilerParams(dimension_semantics=("arbitrary", "arbitrary")),
        name="adaln_rows",
    )(cc, w_ada, b_ada.reshape(depth, 1, n))


def _pre_kernel(x_ref, mod_ref, gpre_ref, win_ref, gq_ref, wuq_ref, gkv_ref, wukv_ref,
                cos_ref, sin_ref, cmg_ref, cmw_ref, cmb_ref,
                q_ref, k_ref, vt_ref, ycm_ref, z_ref, xbc_ref, dt_ref):
    d = x_ref.shape[-1]
    tm = x_ref.shape[0]
    shift = mod_ref[:, 0:d]
    scale = mod_ref[:, d:2 * d]
    lane_group = lax.broadcasted_iota(jnp.int32, (CHUNK, CM_WIDTH), 1) // CM_GROUP_DIM
    sub = min(tm, SUB_ROWS)

    def normed(r0):
        return (_rms(x_ref[r0:r0 + sub, :], gpre_ref[...]) * (1.0 + scale) + shift).astype(BF16)

    h_next = normed(0)
    for r0 in range(0, tm, sub):
        rs_ = slice(r0, r0 + sub)
        h = h_next
        if r0 + sub < tm:
            h_next = normed(r0 + sub)

        def proj(lo, hi):
            return _dot(h, win_ref[:, lo:hi])

        cos = cos_ref[rs_, :]
        sin = sin_ref[rs_, :]

        def rotate(rs):
            return rs * cos + pltpu.roll(rs, ROPE, 1) * sin

        q_lo = proj(P_Q, P_KV)
        kvkr = proj(P_KV, P_CM)
        xcm = proj(P_CM, P_Z)
        z_ref[rs_, :] = proj(P_Z, P_XBC)
        xbc_ref[rs_, :] = proj(P_XBC, P_DT)
        dt_ref[rs_, :] = proj(P_DT, P_END)

        qn = _rms(q_lo, gq_ref[...]).astype(BF16)
        qf = _dot(qn, wuq_ref[...]) * QSCALE
        kvn = _rms(kvkr[:, :KV_LORA], gkv_ref[...]).astype(BF16)
        kvf = _dot(kvn, wukv_ref[...])
        for hh in range(HEADS):
            lo = hh * QK_PAD
            q_ref[rs_, lo:lo + NOPE] = qf[:, lo:lo + NOPE].astype(BF16)
            q_ref[rs_, lo + NOPE:lo + QK_PAD] = rotate(qf[:, lo + NOPE:lo + QK_PAD]).astype(BF16)
        krot = rotate(kvkr[:, KV_LORA:]).astype(BF16)
        for hh in range(HEADS):
            lo = hh * QK_PAD
            k_ref[rs_, lo:lo + NOPE] = kvf[:, hh * (NOPE + VDIM):hh * (NOPE + VDIM) + NOPE].astype(BF16)
            k_ref[rs_, lo + NOPE:lo + QK_PAD] = krot
            vt_ref[hh * VDIM:(hh + 1) * VDIM, rs_] = (
                kvf[:, hh * (NOPE + VDIM) + NOPE:(hh + 1) * (NOPE + VDIM)].T.astype(BF16))

        gl = xcm * (0.5 * (1.0 + jnp.tanh(math.sqrt(2.0 / math.pi) * (xcm + 0.044715 * (xcm * xcm * xcm)))))
        u = gl[:, :CM_WIDTH]
        vv = gl[:, CM_WIDTH:]
        mu = jnp.mean(vv, axis=-1, keepdims=True)
        vc = vv - mu
        var = jnp.mean(vc * vc, axis=-1, keepdims=True)
        vn = vc * lax.rsqrt(var + EPS) * cmg_ref[...]
        for c in range(sub // CHUNK):
            rows = slice(c * CHUNK, (c + 1) * CHUNK)
            vchunk = vn[rows, :]
            vblk = jnp.concatenate(
                [jnp.where(lane_group == g, vchunk, 0.0).astype(BF16) for g in range(CM_GROUPS)], axis=0)
            mixed = _dot(cmw_ref[...], vblk) + cmb_ref[...]
            ycm_ref[r0 + c * CHUNK:r0 + (c + 1) * CHUNK, :] = (u[rows, :] * mixed).astype(BF16)


def _pre_call(x, mod, mod_row, lw, cos, sin, tm):
    b, l, d = x.shape
    grid = (b, l // tm)
    row_spec = lambda w: pl.BlockSpec((None, tm, w), lambda bi, i: (bi, i, 0))
    if mod_row is None:
        mod_spec = pl.BlockSpec((None, 1, mod.shape[-1]), lambda bi, i: (bi, 0, 0))
    else:
        mod_spec = pl.BlockSpec((None, 1, mod.shape[-1]), lambda bi, i: (mod_row, 0, 0))
    tab_spec = pl.BlockSpec((tm, LANE), lambda bi, i: (i, 0))
    weights = [lw["g_pre_mix"], lw["w_in"], lw["g_q"], lw["w_uq"], lw["g_kv"], lw["w_ukv"]]
    cm_weights = [lw["cm_norm_g"], lw["cm_w"], lw["cm_b"]]
    out_widths = [(HEADS * QK_PAD, BF16), (HEADS * QK_PAD, BF16), None, (CM_WIDTH, BF16),
                  (SSD_WIDTH, F32), (SSD_CONV_DIM, F32), (LANE, F32)]
    vt_shape = jax.ShapeDtypeStruct((b, HEADS * VDIM, l), BF16)
    vt_spec = pl.BlockSpec((None, HEADS * VDIM, tm), lambda bi, i: (bi, 0, i))
    return pl.pallas_call(
        _pre_kernel,
        out_shape=[vt_shape if o is None else jax.ShapeDtypeStruct((b, l, o[0]), o[1]) for o in out_widths],
        grid=grid,
        in_specs=[row_spec(d), mod_spec] + [_full_spec(w.shape) for w in weights] + [tab_spec, tab_spec]
        + [_full_spec(w.shape) for w in cm_weights],
        out_specs=[vt_spec if o is None else row_spec(o[0]) for o in out_widths],
        compiler_params=pltpu.CompilerParams(dimension_semantics=("arbitrary", "arbitrary"),
                                             vmem_limit_bytes=VMEM_LIMIT),
        name="pre_mix",
    )(x, mod, *weights, cos, sin, *cm_weights)


def _attn_kernel(q_ref, *refs, tk, rows):
    o_ref, s_scr, p_scr = refs[-3:]
    kv_refs = refs[:-3]
    tq = q_ref.shape[0]
    q = q_ref[...]
    tiles = []
    for seg in range(len(kv_refs) // 2):
        k_ref, vt_ref = kv_refs[2 * seg], kv_refs[2 * seg + 1]
        n = k_ref.shape[0]
        step = min(tk, n)
        tiles += [(k_ref, vt_ref, start, step) for start in range(0, n, step)]

    def scores_into(tile, slot):
        k_ref, _, start, step = tile
        s = _dot_nt(k_ref[start:start + step, :], q)
        s_scr[slot, 0:step, :] = s
        return jnp.max(s, axis=0, keepdims=True)

    m = jnp.full((1, tq), -1e30, F32)
    l8 = jnp.zeros((SUBLANE, tq), F32)
    acc = jnp.zeros((VDIM, tq), F32)
    mx = scores_into(tiles[0], 0)
    for t, (_, vt_ref, start, step) in enumerate(tiles):
        slot = t % 2
        mx_next = scores_into(tiles[t + 1], 1 - slot) if t + 1 < len(tiles) else None
        m_new = jnp.maximum(m, mx)
        alpha = jnp.exp2(m - m_new)
        psum = jnp.zeros((SUBLANE, tq), F32)
        for r in range(0, step, rows):
            pb = jnp.exp2(s_scr[slot, r:r + rows, :] - m_new)
            psum = psum + pb.reshape(rows // SUBLANE, SUBLANE, tq).sum(axis=0)
            p_scr[slot, r:r + rows, :] = pb.astype(BF16)
        l8 = alpha * l8 + psum
        acc = alpha * acc + _dot(vt_ref[:, start:start + step], p_scr[slot, 0:step, :])
        m = m_new
        mx = mx_next
    l = jnp.sum(l8, axis=0, keepdims=True)
    o_ref[...] = (acc * (1.0 / l)).T.astype(BF16)


def _attn_call(q, kvs, tq, tk):
    b, lq, _ = q.shape
    in_specs = [pl.BlockSpec((None, tq, QK_PAD), lambda bi, hi, i: (bi, i, hi))]
    args = [q]
    for k, vt in kvs:
        lk = k.shape[1]
        in_specs.append(pl.BlockSpec((None, lk, QK_PAD), lambda bi, hi, i: (bi, 0, hi)))
        in_specs.append(pl.BlockSpec((None, VDIM, lk), lambda bi, hi, i: (bi, hi, 0)))
        args += [k, vt]
    return pl.pallas_call(
        functools.partial(_attn_kernel, tk=tk, rows=64),
        out_shape=jax.ShapeDtypeStruct((b, lq, HEADS * VDIM), BF16),
        grid=(b, HEADS, lq // tq),
        in_specs=in_specs,
        out_specs=pl.BlockSpec((None, tq, VDIM), lambda bi, hi, i: (bi, i, hi)),
        scratch_shapes=[pltpu.VMEM((2, tk, tq), F32), pltpu.VMEM((2, tk, tq), BF16)],
        compiler_params=pltpu.CompilerParams(dimension_semantics=("arbitrary", "arbitrary", "arbitrary"),
                                             vmem_limit_bytes=VMEM_LIMIT),
        name="mla_attention",
    )(*args)


def _ssd_kernel(xm_f, xp_f, xn_f, dt_f, xm_b, xp_b, xn_b, dt_b,
                convw_ref, convb_ref, dtbias_ref, alog_ref, drep_ref, init_ref,
                yf_ref, yb_ref, fin_ref, st_ref, *, nc):
    j = pl.program_id(1)

    @pl.when(j == 0)
    def _():
        st_ref[...] = init_ref[...]

    nb = xm_f.shape[0]
    a_row = -jnp.exp(alog_ref[...])
    row_id = lax.broadcasted_iota(jnp.int32, (CHUNK, SSD_CONV_DIM), 0)
    ri = lax.broadcasted_iota(jnp.int32, (CHUNK, CHUNK), 0)
    ci = lax.broadcasted_iota(jnp.int32, (CHUNK, CHUNK), 1)
    low_half = ci < (LANE // 2)
    w = convw_ref[...]
    causal = ((ri >= ci), (ri <= ci))
    tri = tuple(jnp.where(cz, 1.0, 0.0).astype(BF16) for cz in causal)
    streams = ((xm_f, xp_f, xn_f, dt_f, yf_ref), (xm_b, xp_b, xn_b, dt_b, yb_ref))
    chains = [(n, d) for n in range(nb) for d in range(2)]
    groups = [(n, d, g) for (n, d) in chains for g in range(SSD_GROUPS)]

    xs, bm, cm, dtv, a_parts = {}, {}, {}, {}, {}
    for n, d in chains:
        xm, xp, xn, dtr, _ = streams[d]
        chunk = j if d == 0 else nc - 1 - j
        x = xm[n]
        prev_row = jnp.where(chunk > 0, xp[n, SUBLANE - 1:SUBLANE, :], 0.0)
        next_row = jnp.where(chunk < nc - 1, xn[n, 0:1, :], 0.0)
        x_prev = jnp.where(row_id == 0, prev_row, pltpu.roll(x, 1, 0))
        x_next = jnp.where(row_id == CHUNK - 1, next_row, pltpu.roll(x, CHUNK - 1, 0))
        xa = _silu(w[0:1, :] * x_prev + w[1:2, :] * x + w[2:3, :] * x_next + convb_ref[...])
        xs[n, d] = xa[:, :SSD_WIDTH]
        bm[n, d] = xa[:, SSD_WIDTH:SSD_WIDTH + SSD_GROUPS * SSD_STATE]
        cm[n, d] = xa[:, SSD_WIDTH + SSD_GROUPS * SSD_STATE:]
        dt_in = dtr[n] + dtbias_ref[...]
        dtv[n, d] = jnp.maximum(dt_in, 0.0) + jnp.log1p(jnp.exp(-jnp.abs(dt_in)))
        a = dtv[n, d] * a_row
        a_hi = a.astype(BF16)
        r1 = a - a_hi.astype(F32)
        a_mid = r1.astype(BF16)
        a_parts[n, d] = (a_hi, a_mid, (r1 - a_mid.astype(F32)).astype(BF16))

    cum = {}
    for n, d in chains:
        a_hi, a_mid, a_lo = a_parts[n, d]
        cum[n, d] = _dot(tri[d], a_hi) + _dot(tri[d], a_mid) + _dot(tri[d], a_lo)

    cm_b, gmat, y_off_raw, st_old = {}, {}, {}, {}
    for n, d, g in groups:
        gl = slice(g * LANE, (g + 1) * LANE)
        cm_b[n, d, g] = cm[n, d][:, gl].astype(BF16)
        gmat[n, d, g] = _dot_nt(cm_b[n, d, g], bm[n, d][:, gl].astype(BF16))
        st_old[n, d, g] = st_ref[n, d, g]
        y_off_raw[n, d, g] = _dot(cm_b[n, d, g], st_old[n, d, g].astype(BF16))

    cum_t, tot = {}, {}
    for n, d in chains:
        cum_t[n, d] = cum[n, d].T
        tot[n, d] = cum[n, d][CHUNK - 1:CHUNK, :] if d == 0 else cum[n, d][0:1, :]
    colx, totx, xdt_b, xdec_b, scores = {}, {}, {}, {}, {}
    for n, d, g in groups:
        i0 = d * SSD_HEADS + 2 * g
        i1 = i0 + 1
        gl = slice(g * LANE, (g + 1) * LANE)
        c_nd = cum[n, d]
        colx[n, d, g] = jnp.where(low_half, c_nd[:, i0:i0 + 1], c_nd[:, i1:i1 + 1])
        dtx = jnp.where(low_half, dtv[n, d][:, i0:i0 + 1], dtv[n, d][:, i1:i1 + 1])
        totx[n, d, g] = jnp.where(low_half[0:1, :], tot[n, d][:, i0:i0 + 1], tot[n, d][:, i1:i1 + 1])
        xdt = xs[n, d][:, gl] * dtx
        xdt_b[n, d, g] = xdt.astype(BF16)
        xdec_b[n, d, g] = (xdt * jnp.exp(totx[n, d, g] - colx[n, d, g])).astype(BF16)
        for hh, ii in enumerate((i0, i1)):
            diff = c_nd[:, ii:ii + 1] - cum_t[n, d][ii:ii + 1, :]
            lmat = jnp.exp(jnp.where(causal[d], diff, -1e30))
            scores[n, d, g, hh] = (gmat[n, d, g] * lmat).astype(BF16)

    ys, st_add = {}, {}
    for n, d, g in groups:
        gl = slice(g * LANE, (g + 1) * LANE)
        ys[n, d, g] = [_dot(scores[n, d, g, hh], xdt_b[n, d, g]) for hh in range(2)]
        st_add[n, d, g] = _dot(bm[n, d][:, gl].T.astype(BF16), xdec_b[n, d, g])

    for n, d, g in groups:
        gl = slice(g * LANE, (g + 1) * LANE)
        y = jnp.where(low_half, ys[n, d, g][0], ys[n, d, g][1]) + y_off_raw[n, d, g] * jnp.exp(colx[n, d, g])
        if d == 0:
            y = y + (drep_ref[0:1, gl] + drep_ref[1:2, gl]) * xs[n, d][:, gl]
        streams[d][4][n, :, gl] = y
        st_ref[n, d, g] = jnp.exp(totx[n, d, g]) * st_old[n, d, g] + st_add[n, d, g]

    @pl.when(j == nc - 1)
    def _():
        fin_ref[...] = st_ref[...]


def _ssd_call(xbc, dt, lw, init, nb_want=4):
    b, l, _ = xbc.shape
    nc = l // CHUNK
    per8 = CHUNK // SUBLANE
    last8 = l // SUBLANE - 1
    fwd = lambda bi, j: (bi, j, 0)
    bwd = lambda bi, j: (bi, nc - 1 - j, 0)
    fwd_prev = lambda bi, j: (bi, jnp.maximum(j * per8 - 1, 0), 0)
    fwd_next = lambda bi, j: (bi, jnp.minimum((j + 1) * per8, last8), 0)
    bwd_prev = lambda bi, j: (bi, jnp.maximum((nc - 1 - j) * per8 - 1, 0), 0)
    bwd_next = lambda bi, j: (bi, jnp.minimum((nc - j) * per8, last8), 0)
    nb = nb_want if b % nb_want == 0 else 1
    main = lambda f: pl.BlockSpec((nb, CHUNK, SSD_CONV_DIM), f)
    halo = lambda f: pl.BlockSpec((nb, SUBLANE, SSD_CONV_DIM), f)
    dts = lambda f: pl.BlockSpec((nb, CHUNK, LANE), f)
    small = [lw["ssd_conv_w"], lw["ssd_conv_b"], lw["ssd_dt_bias"], lw["ssd_a_log"], lw["ssd_d"]]
    st_shape = (2, SSD_GROUPS, SSD_STATE, LANE)
    st_spec = pl.BlockSpec((nb,) + st_shape, lambda bi, j: (bi, 0, 0, 0, 0))
    y_shape = jax.ShapeDtypeStruct((b, l, SSD_WIDTH), F32)
    return pl.pallas_call(
        functools.partial(_ssd_kernel, nc=nc),
        out_shape=[y_shape, y_shape, jax.ShapeDtypeStruct((b,) + st_shape, F32)],
        grid=(b // nb, nc),
        in_specs=[main(fwd), halo(fwd_prev), halo(fwd_next), dts(fwd),
                  main(bwd), halo(bwd_prev), halo(bwd_next), dts(bwd)]
        + [_full_spec(w.shape) for w in small] + [st_spec],
        out_specs=[pl.BlockSpec((nb, CHUNK, SSD_WIDTH), fwd), pl.BlockSpec((nb, CHUNK, SSD_WIDTH), bwd), st_spec],
        scratch_shapes=[pltpu.VMEM((nb,) + st_shape, F32)],
        compiler_params=pltpu.CompilerParams(dimension_semantics=("arbitrary", "arbitrary"),
                                             vmem_limit_bytes=VMEM_LIMIT),
        name="ssd_scan",
    )(xbc, xbc, xbc, dt, xbc, xbc, xbc, dt, *small, init)


def _post_kernel(x_ref, ya_ref, ycm_ref, yf_ref, yb_ref, z_ref, mod_ref,
                 gssd_ref, gpm_ref, gpf_ref, gpo_ref, wout_ref, w1_ref, w2_ref, o_ref, *, tf):
    d = x_ref.shape[-1]
    f = w1_ref.shape[-1]
    gate1 = mod_ref[:, 2 * d:3 * d]
    shift2 = mod_ref[:, 3 * d:4 * d]
    scale2 = mod_ref[:, 4 * d:5 * d]
    gate2 = mod_ref[:, 5 * d:6 * d]
    wa = HEADS * VDIM
    tm = x_ref.shape[0]
    sub = min(tm, SUB_ROWS)
    def mixed(r0):
        rs_ = slice(r0, r0 + sub)
        y_ssd = _rms((yf_ref[rs_, :] + yb_ref[rs_, :]) * _silu(z_ref[rs_, :]), gssd_ref[...]).astype(BF16)
        y = (_dot(ya_ref[rs_, :], wout_ref[0:wa, :]) + _dot(ycm_ref[rs_, :], wout_ref[wa:wa + CM_WIDTH, :])
             + _dot(y_ssd, wout_ref[wa + CM_WIDTH:, :]))
        x1 = x_ref[rs_, :] + gate1 * _rms(y, gpm_ref[...])
        return x1, (_rms(x1, gpf_ref[...]) * (1.0 + scale2) + shift2).astype(BF16)

    nxt = mixed(0)
    for r0 in range(0, tm, sub):
        rs_ = slice(r0, r0 + sub)
        x1, h2 = nxt
        if r0 + sub < tm:
            nxt = mixed(r0 + sub)
        acc = jnp.zeros(x1.shape, F32)
        for c in range(f // tf):
            hc = jnp.maximum(_dot(h2, w1_ref[:, c * tf:(c + 1) * tf]), 0.0)
            acc = acc + _dot((hc * hc).astype(BF16), w2_ref[c * tf:(c + 1) * tf, :])
        o_ref[rs_, :] = x1 + gate2 * _rms(acc, gpo_ref[...])


def _post_call(x, ya, ycm, yf, yb, z, mod, mod_row, lw, tm):
    b, l, d = x.shape
    row_spec = lambda w: pl.BlockSpec((None, tm, w), lambda bi, i: (bi, i, 0))
    if mod_row is None:
        mod_spec = pl.BlockSpec((None, 1, mod.shape[-1]), lambda bi, i: (bi, 0, 0))
    else:
        mod_spec = pl.BlockSpec((None, 1, mod.shape[-1]), lambda bi, i: (mod_row, 0, 0))
    weights = [lw["ssd_norm_g"], lw["g_post_mix"], lw["g_pre_ff"], lw["g_post_ff"], lw["w_out"], lw["w_ff1"], lw["w_ff2"]]
    single = lambda w: pl.BlockSpec(w.shape, lambda *_: (0,) * w.ndim, pipeline_mode=pl.Buffered(1))
    return pl.pallas_call(
        functools.partial(_post_kernel, tf=1024),
        out_shape=jax.ShapeDtypeStruct((b, l, d), F32),
        grid=(b, l // tm),
        in_specs=[row_spec(d), row_spec(HEADS * VDIM), row_spec(CM_WIDTH), row_spec(SSD_WIDTH), row_spec(SSD_WIDTH),
                  row_spec(SSD_WIDTH), mod_spec] + [single(w) for w in weights],
        out_specs=row_spec(d),
        compiler_params=pltpu.CompilerParams(dimension_semantics=("arbitrary", "arbitrary"),
                                             vmem_limit_bytes=VMEM_LIMIT),
        name="post_mix_ffn",
    )(x, ya, ycm, yf, yb, z, mod, *weights)


def _rope_tables(n_tokens):
    rows_n = n_tokens // GRID_W
    row = jnp.repeat(jnp.arange(rows_n, dtype=F32), GRID_W)
    col = jnp.tile(jnp.arange(GRID_W, dtype=F32), rows_n)
    axis_dim = ROPE // 2
    inv_freq = ROPE_BASE ** (-jnp.arange(0, axis_dim, 2, dtype=F32) / axis_dim)
    ang_r = row[:, None] * inv_freq
    ang_c = col[:, None] * inv_freq
    zeros = jnp.zeros((n_tokens, LANE - ROPE), F32)
    cos = jnp.concatenate([jnp.cos(ang_r), jnp.cos(ang_r), jnp.cos(ang_c), jnp.cos(ang_c), zeros], axis=-1)
    sin = jnp.concatenate([-jnp.sin(ang_r), jnp.sin(ang_r), -jnp.sin(ang_c), jnp.sin(ang_c), zeros], axis=-1)
    return cos, sin


def _rotate_half_partner():
    q = ROPE // 4
    return np.concatenate([np.arange(q, 2 * q), np.arange(0, q), np.arange(3 * q, 4 * q), np.arange(2 * q, 3 * q)])


def _pack_weights(p):
    depth = p["w_in"].shape[0]
    partner = _rotate_half_partner()
    w_in = p["w_in"]
    kr = w_in[..., 384:448]
    pad = jnp.zeros(w_in.shape[:-1] + (LANE - 2 * SSD_HEADS,), w_in.dtype)
    w_in_p = jnp.concatenate([w_in[..., :384], kr, kr[..., partner], w_in[..., 448:1984], w_in[..., 1984:], pad], axis=-1)
    w_uq = p["w_uq"].reshape(depth, Q_LORA, HEADS, NOPE + ROPE)
    rope_cols = w_uq[..., NOPE:]
    w_uq_p = jnp.concatenate([w_uq[..., :NOPE], rope_cols, rope_cols[..., partner]], axis=-1)
    row = lambda a: a.reshape(depth, 1, -1)
    pad_lane = lambda a: jnp.pad(a.reshape(depth, 1, -1), ((0, 0), (0, 0), (0, LANE - 2 * SSD_HEADS)))
    return {
        "g_pre_mix": row(p["g_pre_mix"]), "g_post_mix": row(p["g_post_mix"]),
        "g_pre_ff": row(p["g_pre_ff"]), "g_post_ff": row(p["g_post_ff"]),
        "w_in": w_in_p.astype(BF16),
        "g_q": row(p["g_q"]), "w_uq": w_uq_p.reshape(depth, Q_LORA, HEADS * QK_PAD).astype(BF16),
        "g_kv": row(p["g_kv"]), "w_ukv": p["w_ukv"].astype(BF16),
        "cm_norm_g": row(p["cm_norm_g"]),
        "cm_w": jnp.transpose(p["cm_w_s"], (0, 2, 1, 3)).reshape(depth, CHUNK, CM_GROUPS * CHUNK).astype(BF16),
        "cm_b": jnp.repeat(jnp.transpose(p["cm_b_s"], (0, 2, 1)), CM_GROUP_DIM, axis=-1),
        "ssd_conv_w": p["ssd_conv_w"], "ssd_conv_b": row(p["ssd_conv_b"]),
        "ssd_dt_bias": pad_lane(p["ssd_dt_bias"]), "ssd_a_log": pad_lane(p["ssd_a_log"]),
        "ssd_d": jnp.repeat(p["ssd_d"], SSD_WIDTH // SSD_HEADS, axis=-1),
        "ssd_norm_g": row(p["ssd_norm_g"]),
        "w_out": p["w_out"].astype(BF16), "w_ff1": p["w_ff1"].astype(BF16), "w_ff2": p["w_ff2"].astype(BF16),
    }


def _row_tile(l, want):
    return want if l % want == 0 else l


def kernel(x, c, ctx, c_ctx, w_ada, b_ada, g_pre_mix, g_post_mix, g_pre_ff, g_post_ff, w_in, g_q, w_uq, g_kv, w_ukv, cm_norm_g, cm_w_s, cm_b_s, ssd_conv_w, ssd_conv_b, ssd_dt_bias, ssd_a_log, ssd_d, ssd_norm_g, w_out, w_ff1, w_ff2):
    b, n_lat, d = x.shape
    n_ctx = ctx.shape[1]
    depth = w_ada.shape[0]
    assert b + 1 <= MOD_ROWS and n_lat % CHUNK == 0 and n_ctx % CHUNK == 0 and n_lat % GRID_W == 0

    packed = _pack_weights(dict(
        g_pre_mix=g_pre_mix, g_post_mix=g_post_mix, g_pre_ff=g_pre_ff, g_post_ff=g_post_ff, w_in=w_in, g_q=g_q,
        w_uq=w_uq, g_kv=g_kv, w_ukv=w_ukv, cm_norm_g=cm_norm_g, cm_w_s=cm_w_s, cm_b_s=cm_b_s,
        ssd_conv_w=ssd_conv_w, ssd_conv_b=ssd_conv_b, ssd_dt_bias=ssd_dt_bias, ssd_a_log=ssd_a_log, ssd_d=ssd_d,
        ssd_norm_g=ssd_norm_g, w_out=w_out, w_ff1=w_ff1, w_ff2=w_ff2))

    cc = jnp.concatenate([c, c_ctx[None, :], jnp.zeros((MOD_ROWS - b - 1, d), c.dtype)], axis=0)
    mods = _mod_call(cc, w_ada, b_ada)

    cos_l, sin_l = _rope_tables(n_lat)
    cos_c = jnp.concatenate([jnp.ones((n_ctx, ROPE), F32), jnp.zeros((n_ctx, LANE - ROPE), F32)], axis=-1)
    sin_c = jnp.zeros((n_ctx, LANE), F32)

    tm_l = _row_tile(n_lat, 1024)
    tm_c = _row_tile(n_ctx, 256)
    tq_l = _row_tile(n_lat, 1024)
    tk = 512
    zero_states = jnp.zeros((b, 2, SSD_GROUPS, SSD_STATE, LANE), F32)

    xc = ctx
    for layer in range(depth):
        last = layer == depth - 1
        lw = {name: w[layer] for name, w in packed.items()}
        mod = mods[layer].reshape(MOD_ROWS, 1, 6 * d)

        cq, ck, cv, cycm, cz, cxbc, cdt = _pre_call(xc, mod, b, lw, cos_c, sin_c, tm_c)
        cyf, cyb, ctx_states = _ssd_call(cxbc, cdt, lw, zero_states)
        if not last:
            cya = _attn_call(cq, [(ck, cv)], tm_c, tk)
            xc = _post_call(xc, cya, cycm, cyf, cyb, cz, mod, b, lw, tm_c)

        q, k, v, ycm, z, xbc, dt = _pre_call(x, mod, None, lw, cos_l, sin_l, tm_l)
        yf, yb, _ = _ssd_call(xbc, dt, lw, ctx_states)
        ya = _attn_call(q, [(k, v), (ck, cv)], tq_l, tk)
        x = _post_call(x, ya, ycm, yf, yb, z, mod, None, lw, tm_l)
    return x
```

```python
import functools
import math

import numpy as np
import jax
import jax.numpy as jnp
from jax import lax
from jax.experimental import pallas as pl
from jax.experimental.pallas import tpu as pltpu

F32 = jnp.float32
BF16 = jnp.bfloat16

EPS = 1e-6
GRID_W = 64
ROPE_BASE = 10000.0

HEADS = 4
Q_LORA = 256
KV_LORA = 128
NOPE = 128
ROPE = 64
VDIM = 128
QK_PAD = 256
ATTN_SCALE = (NOPE + ROPE) ** -0.5
QSCALE = ATTN_SCALE * math.log2(math.e)

CM_GROUPS = 4
CM_WIDTH = 256
CM_GROUP_DIM = CM_WIDTH // CM_GROUPS
CHUNK = 128

SSD_WIDTH = 256
SSD_HEADS = 4
SSD_GROUPS = 2
SSD_STATE = 128
SSD_CONV_DIM = SSD_WIDTH + 2 * SSD_GROUPS * SSD_STATE

LANE = 128
SUBLANE = 8
HALO = 16
MOD_ROWS = 16
SUB_ROWS = 256

P_Q, P_KV, P_KR, P_CM, P_Z, P_XBC, P_DT, P_END = 0, 256, 384, 512, 1024, 1280, 2048, 2176

VMEM_LIMIT = 56 * 1024 * 1024


def _rms(x, g):
    ms = jnp.mean(x * x, axis=-1, keepdims=True)
    return x * lax.rsqrt(ms + EPS) * g


def _sigmoid(x):
    return 1.0 / (1.0 + jnp.exp(-x))


def _silu(x):
    return x * _sigmoid(x)


def _dot(a, b):
    return jnp.dot(a, b, preferred_element_type=F32)


def _dot_nt(a, b):
    return lax.dot_general(a, b, (((1,), (1,)), ((), ())), preferred_element_type=F32)


def _layer_spec(w, layer, single=False):
    idx = (layer,) + (0,) * (w.ndim - 1)
    mode = dict(pipeline_mode=pl.Buffered(1)) if single else {}
    return pl.BlockSpec((None,) + w.shape[1:], lambda *_: idx, **mode)


def _mod_spec(mods, layer, mod_row):
    if mod_row is None:
        return pl.BlockSpec((None, None, 1, mods.shape[-1]), lambda bi, i: (layer, bi, 0, 0))
    return pl.BlockSpec((None, None, 1, mods.shape[-1]), lambda bi, i: (layer, mod_row, 0, 0))


def _mod_kernel(c_ref, w_ref, b_ref, o_ref):
    s = _silu(c_ref[...]).astype(BF16)
    o_ref[...] = _dot(s, w_ref[...].astype(BF16)) + b_ref[...]


def _mod_call(cc, w_ada, b_ada):
    depth, d, n = w_ada.shape
    tn = 512
    return pl.pallas_call(
        _mod_kernel,
        out_shape=jax.ShapeDtypeStruct((depth, MOD_ROWS, n), F32),
        grid=(depth, n // tn),
        in_specs=[
            pl.BlockSpec((MOD_ROWS, d), lambda l, j: (0, 0)),
            pl.BlockSpec((None, d, tn), lambda l, j: (l, 0, j)),
            pl.BlockSpec((None, 1, tn), lambda l, j: (l, 0, j)),
        ],
        out_specs=pl.BlockSpec((None, MOD_ROWS, tn), lambda l, j: (l, 0, j)),
        compiler_params=pltpu.CompilerParams(dimension_semantics=("arbitrary", "arbitrary")),
        name="adaln_rows",
    )(cc, w_ada, b_ada.reshape(depth, 1, n))


def _pre_kernel(x_ref, xp_ref, xn_ref, mod_ref, gpre_ref, win_ref, gq_ref, wuq_ref, gkv_ref, wukv_ref,
                cos_ref, sin_ref, cmg_ref, cmw_ref, cmb_ref, convw_ref, convb_ref,
                q_ref, k_ref, vt_ref, ycm_ref, z_ref, xs_ref, cmo_ref, bmt_ref, dt_ref, h_scr):
    d = x_ref.shape[-1]
    tm = x_ref.shape[0]
    i = pl.program_id(1)
    last_tile = pl.num_programs(1) - 1
    shift = mod_ref[:, 0:d]
    scale = mod_ref[:, d:2 * d]
    lane_group = lax.broadcasted_iota(jnp.int32, (CHUNK, CM_WIDTH), 1) // CM_GROUP_DIM
    sub = min(tm, SUB_ROWS)
    row_id = lax.broadcasted_iota(jnp.int32, (sub, SSD_CONV_DIM), 0)
    cw = convw_ref[...]

    def normed(xv):
        return (_rms(xv, gpre_ref[...]) * (1.0 + scale) + shift).astype(BF16)

    h_scr[0:HALO, :] = normed(xp_ref[...])
    h_scr[HALO + tm:2 * HALO + tm, :] = normed(xn_ref[...])
    h_scr[HALO:HALO + sub, :] = normed(x_ref[0:sub, :])
    for r0 in range(0, tm, sub):
        rs_ = slice(r0, r0 + sub)
        if r0 + sub < tm:
            h_scr[HALO + r0 + sub:HALO + r0 + 2 * sub, :] = normed(x_ref[r0 + sub:r0 + 2 * sub, :])
        h = h_scr[HALO + r0:HALO + r0 + sub, :]

        def proj(lo, hi):
            return _dot(h, win_ref[:, lo:hi])

        cos = cos_ref[rs_, :]
        sin = sin_ref[rs_, :]

        def rotate(rs):
            return rs * cos + pltpu.roll(rs, ROPE, 1) * sin

        q_lo = proj(P_Q, P_KV)
        kvkr = proj(P_KV, P_CM)
        xcm = proj(P_CM, P_Z)
        xbc = _dot(h_scr[r0:r0 + sub + 2 * HALO, :], win_ref[:, P_XBC:P_DT])
        z_ref[rs_, :] = proj(P_Z, P_XBC)
        dt_ref[rs_, :] = proj(P_DT, P_END)

        qn = _rms(q_lo, gq_ref[...]).astype(BF16)
        qf = _dot(qn, wuq_ref[...]) * QSCALE
        kvn = _rms(kvkr[:, :KV_LORA], gkv_ref[...]).astype(BF16)
        kvf = _dot(kvn, wukv_ref[...])
        for hh in range(HEADS):
            lo = hh * QK_PAD
            q_ref[rs_, lo:lo + NOPE] = qf[:, lo:lo + NOPE].astype(BF16)
            q_ref[rs_, lo + NOPE:lo + QK_PAD] = rotate(qf[:, lo + NOPE:lo + QK_PAD]).astype(BF16)
        krot = rotate(kvkr[:, KV_LORA:]).astype(BF16)
        for hh in range(HEADS):
            lo = hh * QK_PAD
            k_ref[rs_, lo:lo + NOPE] = kvf[:, hh * (NOPE + VDIM):hh * (NOPE + VDIM) + NOPE].astype(BF16)
            k_ref[rs_, lo + NOPE:lo + QK_PAD] = krot
            vt_ref[hh * VDIM:(hh + 1) * VDIM, rs_] = (
                kvf[:, hh * (NOPE + VDIM) + NOPE:(hh + 1) * (NOPE + VDIM)].T.astype(BF16))

        n_ext = sub + 2 * HALO
        x_prev = pltpu.roll(xbc, 1, 0)[HALO:HALO + sub, :]
        x_next = pltpu.roll(xbc, n_ext - 1, 0)[HALO:HALO + sub, :]
        if r0 == 0:
            x_prev = jnp.where((row_id == 0) & (i == 0), 0.0, x_prev)
        if r0 + sub == tm:
            x_next = jnp.where((row_id == sub - 1) & (i == last_tile), 0.0, x_next)
        xa = _silu(cw[0:1, :] * x_prev + cw[1:2, :] * xbc[HALO:HALO + sub, :] + cw[2:3, :] * x_next + convb_ref[...])
        xs_ref[rs_, :] = xa[:, :SSD_WIDTH]
        cmo_ref[rs_, :] = xa[:, SSD_WIDTH + SSD_GROUPS * SSD_STATE:].astype(BF16)
        for g in range(SSD_GROUPS):
            for c in range(sub // CHUNK):
                blk = xa[c * CHUNK:(c + 1) * CHUNK, SSD_WIDTH + g * SSD_STATE:SSD_WIDTH + (g + 1) * SSD_STATE]
                bmt_ref[g * SSD_STATE:(g + 1) * SSD_STATE, r0 + c * CHUNK:r0 + (c + 1) * CHUNK] = blk.T.astype(BF16)

        gl = xcm * (0.5 * (1.0 + jnp.tanh(math.sqrt(2.0 / math.pi) * (xcm + 0.044715 * (xcm * xcm * xcm)))))
        u = gl[:, :CM_WIDTH]
        vv = gl[:, CM_WIDTH:]
        mu = jnp.mean(vv, axis=-1, keepdims=True)
        vc = vv - mu
        var = jnp.mean(vc * vc, axis=-1, keepdims=True)
        vn = vc * lax.rsqrt(var + EPS) * cmg_ref[...]
        for c in range(sub // CHUNK):
            rows = slice(c * CHUNK, (c + 1) * CHUNK)
            vchunk = vn[rows, :]
            vblk = jnp.concatenate(
                [jnp.where(lane_group == g, vchunk, 0.0).astype(BF16) for g in range(CM_GROUPS)], axis=0)
            mixed = _dot(cmw_ref[...], vblk) + cmb_ref[...]
            ycm_ref[r0 + c * CHUNK:r0 + (c + 1) * CHUNK, :] = (u[rows, :] * mixed).astype(BF16)


def _pre_call(x, mods, layer, mod_row, pk, cos, sin, tm):
    b, l, d = x.shape
    nt = l // tm
    per = tm // HALO
    last_halo = l // HALO - 1
    row_spec = lambda w: pl.BlockSpec((None, tm, w), lambda bi, i: (bi, i, 0))
    col_spec = lambda h: pl.BlockSpec((None, h, tm), lambda bi, i: (bi, 0, i))
    halo_prev = pl.BlockSpec((None, HALO, d), lambda bi, i: (bi, jnp.maximum(i * per - 1, 0), 0))
    halo_next = pl.BlockSpec((None, HALO, d), lambda bi, i: (bi, jnp.minimum((i + 1) * per, last_halo), 0))
    tab_spec = pl.BlockSpec((tm, LANE), lambda bi, i: (i, 0))
    weights = [pk[n] for n in ("g_pre_mix", "w_in", "g_q", "w_uq", "g_kv", "w_ukv")]
    tail_weights = [pk[n] for n in ("cm_norm_g", "cm_w", "cm_b", "ssd_conv_w", "ssd_conv_b")]
    outs = [((l, HEADS * QK_PAD), BF16, row_spec(HEADS * QK_PAD)), ((l, HEADS * QK_PAD), BF16, row_spec(HEADS * QK_PAD)),
            ((HEADS * VDIM, l), BF16, col_spec(HEADS * VDIM)), ((l, CM_WIDTH), BF16, row_spec(CM_WIDTH)),
            ((l, SSD_WIDTH), F32, row_spec(SSD_WIDTH)), ((l, SSD_WIDTH), F32, row_spec(SSD_WIDTH)),
            ((l, SSD_GROUPS * SSD_STATE), BF16, row_spec(SSD_GROUPS * SSD_STATE)),
            ((SSD_GROUPS * SSD_STATE, l), BF16, col_spec(SSD_GROUPS * SSD_STATE)), ((l, LANE), F32, row_spec(LANE))]
    return pl.pallas_call(
        _pre_kernel,
        out_shape=[jax.ShapeDtypeStruct((b,) + s, dt) for s, dt, _ in outs],
        grid=(b, nt),
        in_specs=[row_spec(d), halo_prev, halo_next, _mod_spec(mods, layer, mod_row)]
        + [_layer_spec(w, layer) for w in weights] + [tab_spec, tab_spec]
        + [_layer_spec(w, layer) for w in tail_weights],
        out_specs=[sp for _, _, sp in outs],
        scratch_shapes=[pltpu.VMEM((tm + 2 * HALO, d), BF16)],
        compiler_params=pltpu.CompilerParams(dimension_semantics=("arbitrary", "arbitrary"),
                                             vmem_limit_bytes=VMEM_LIMIT),
        name="pre_mix",
    )(x, x, x, mods, *weights, cos, sin, *tail_weights)


def _attn_kernel(q_ref, *refs, tk, rows):
    o_ref, s_scr, p_scr = refs[-3:]
    kv_refs = refs[:-3]
    tq = q_ref.shape[0]
    q = q_ref[...]
    tiles = []
    for seg in range(len(kv_refs) // 2):
        k_ref, vt_ref = kv_refs[2 * seg], kv_refs[2 * seg + 1]
        n = k_ref.shape[0]
        step = min(tk, n)
        tiles += [(k_ref, vt_ref, start, step) for start in range(0, n, step)]

    def scores_into(tile, slot):
        k_ref, _, start, step = tile
        s = _dot_nt(k_ref[start:start + step, :], q)
        s_scr[slot, 0:step, :] = s
        return jnp.max(s, axis=0, keepdims=True)

    m = jnp.full((1, tq), -1e30, F32)
    l8 = jnp.zeros((SUBLANE, tq), F32)
    acc = jnp.zeros((VDIM, tq), F32)
    mx = scores_into(tiles[0], 0)
    for t, (_, vt_ref, start, step) in enumerate(tiles):
        slot = t % 2
        mx_next = scores_into(tiles[t + 1], 1 - slot) if t + 1 < len(tiles) else None
        m_new = jnp.maximum(m, mx)
        alpha = jnp.exp2(m - m_new)
        psum = jnp.zeros((SUBLANE, tq), F32)
        for r in range(0, step, rows):
            pb = jnp.exp2(s_scr[slot, r:r + rows, :] - m_new)
            psum = psum + pb.reshape(rows // SUBLANE, SUBLANE, tq).sum(axis=0)
            p_scr[slot, r:r + rows, :] = pb.astype(BF16)
        l8 = alpha * l8 + psum
        acc = alpha * acc + _dot(vt_ref[:, start:start + step], p_scr[slot, 0:step, :])
        m = m_new
        mx = mx_next
    l = jnp.sum(l8, axis=0, keepdims=True)
    o_ref[...] = (acc * (1.0 / l)).T.astype(BF16)


def _attn_call(q, kvs, tq, tk):
    b, lq, _ = q.shape
    in_specs = [pl.BlockSpec((None, tq, QK_PAD), lambda bi, hi, i: (bi, i, hi))]
    args = [q]
    for k, vt in kvs:
        lk = k.shape[1]
        in_specs.append(pl.BlockSpec((None, lk, QK_PAD), lambda bi, hi, i: (bi, 0, hi)))
        in_specs.append(pl.BlockSpec((None, VDIM, lk), lambda bi, hi, i: (bi, hi, 0)))
        args += [k, vt]
    return pl.pallas_call(
        functools.partial(_attn_kernel, tk=tk, rows=64),
        out_shape=jax.ShapeDtypeStruct((b, lq, HEADS * VDIM), BF16),
        grid=(b, HEADS, lq // tq),
        in_specs=in_specs,
        out_specs=pl.BlockSpec((None, tq, VDIM), lambda bi, hi, i: (bi, i, hi)),
        scratch_shapes=[pltpu.VMEM((2, tk, tq), F32), pltpu.VMEM((2, tk, tq), BF16)],
        compiler_params=pltpu.CompilerParams(dimension_semantics=("arbitrary", "arbitrary", "arbitrary"),
                                             vmem_limit_bytes=VMEM_LIMIT),
        name="mla_attention",
    )(*args)


def _ssd_kernel(xs_f, cm_f, bmt_f, dt_f, xs_b, cm_b_, bmt_b, dt_b,
                dtbias_ref, alog_ref, drep_ref, init_ref,
                yf_ref, yb_ref, fin_ref, st_ref, *, nc):
    j = pl.program_id(1)

    @pl.when(j == 0)
    def _():
        st_ref[...] = init_ref[...]

    nb = xs_f.shape[0]
    a_row = -jnp.exp(alog_ref[...])
    ri = lax.broadcasted_iota(jnp.int32, (CHUNK, CHUNK), 0)
    ci = lax.broadcasted_iota(jnp.int32, (CHUNK, CHUNK), 1)
    low_half = ci < (LANE // 2)
    causal = ((ri >= ci), (ri <= ci))
    tri = tuple(jnp.where(cz, 1.0, 0.0).astype(BF16) for cz in causal)
    streams = ((xs_f, cm_f, bmt_f, dt_f, yf_ref), (xs_b, cm_b_, bmt_b, dt_b, yb_ref))
    chains = [(n, d) for n in range(nb) for d in range(2)]
    groups = [(n, d, g) for (n, d) in chains for g in range(SSD_GROUPS)]

    dtv, a_parts = {}, {}
    for n, d in chains:
        dt_in = streams[d][3][n] + dtbias_ref[...]
        dtv[n, d] = jnp.maximum(dt_in, 0.0) + jnp.log1p(jnp.exp(-jnp.abs(dt_in)))
        a = dtv[n, d] * a_row
        a_hi = a.astype(BF16)
        r1 = a - a_hi.astype(F32)
        a_mid = r1.astype(BF16)
        a_parts[n, d] = (a_hi, a_mid, (r1 - a_mid.astype(F32)).astype(BF16))

    cum = {}
    for n, d in chains:
        a_hi, a_mid, a_lo = a_parts[n, d]
        cum[n, d] = _dot(tri[d], a_hi) + _dot(tri[d], a_mid) + _dot(tri[d], a_lo)

    cmb, gmat, y_off_raw, st_old = {}, {}, {}, {}
    for n, d, g in groups:
        gl = slice(g * LANE, (g + 1) * LANE)
        cmb[n, d, g] = streams[d][1][n, :, gl]
        gmat[n, d, g] = _dot(cmb[n, d, g], streams[d][2][n, g * SSD_STATE:(g + 1) * SSD_STATE, :])
        st_old[n, d, g] = st_ref[n, d, g]
        y_off_raw[n, d, g] = _dot(cmb[n, d, g], st_old[n, d, g].astype(BF16))

    cum_t, tot = {}, {}
    for n, d in chains:
        cum_t[n, d] = cum[n, d].T
        tot[n, d] = cum[n, d][CHUNK - 1:CHUNK, :] if d == 0 else cum[n, d][0:1, :]
    colx, totx, xdt_b, xdec_b, scores = {}, {}, {}, {}, {}
    for n, d, g in groups:
        i0 = d * SSD_HEADS + 2 * g
        i1 = i0 + 1
        gl = slice(g * LANE, (g + 1) * LANE)
        c_nd = cum[n, d]
        colx[n, d, g] = jnp.where(low_half, c_nd[:, i0:i0 + 1], c_nd[:, i1:i1 + 1])
        dtx = jnp.where(low_half, dtv[n, d][:, i0:i0 + 1], dtv[n, d][:, i1:i1 + 1])
        totx[n, d, g] = jnp.where(low_half[0:1, :], tot[n, d][:, i0:i0 + 1], tot[n, d][:, i1:i1 + 1])
        xdt = streams[d][0][n, :, gl] * dtx
        xdt_b[n, d, g] = xdt.astype(BF16)
        xdec_b[n, d, g] = (xdt * jnp.exp(totx[n, d, g] - colx[n, d, g])).astype(BF16)
        for hh, ii in enumerate((i0, i1)):
            diff = c_nd[:, ii:ii + 1] - cum_t[n, d][ii:ii + 1, :]
            lmat = jnp.exp(jnp.where(causal[d], diff, -1e30))
            scores[n, d, g, hh] = (gmat[n, d, g] * lmat).astype(BF16)

    ys, st_add = {}, {}
    for n, d, g in groups:
        ys[n, d, g] = [_dot(scores[n, d, g, hh], xdt_b[n, d, g]) for hh in range(2)]
        st_add[n, d, g] = _dot(streams[d][2][n, g * SSD_STATE:(g + 1) * SSD_STATE, :], xdec_b[n, d, g])

    for n, d, g in groups:
        gl = slice(g * LANE, (g + 1) * LANE)
        y = jnp.where(low_half, ys[n, d, g][0], ys[n, d, g][1]) + y_off_raw[n, d, g] * jnp.exp(colx[n, d, g])
        if d == 0:
            y = y + (drep_ref[0:1, gl] + drep_ref[1:2, gl]) * streams[d][0][n, :, gl]
        streams[d][4][n, :, gl] = y
        st_ref[n, d, g] = jnp.exp(totx[n, d, g]) * st_old[n, d, g] + st_add[n, d, g]

    @pl.when(j == nc - 1)
    def _():
        fin_ref[...] = st_ref[...]


def _ssd_call(xs, cmo, bmt, dt, pk, layer, init, nb_want=4):
    b, l, _ = xs.shape
    nc = l // CHUNK
    fwd = lambda bi, j: (bi, j, 0)
    bwd = lambda bi, j: (bi, nc - 1 - j, 0)
    fwd_t = lambda bi, j: (bi, 0, j)
    bwd_t = lambda bi, j: (bi, 0, nc - 1 - j)
    nb = nb_want if b % nb_want == 0 else 1
    rows = lambda w, f: pl.BlockSpec((nb, CHUNK, w), f)
    cols = lambda h, f: pl.BlockSpec((nb, h, CHUNK), f)
    small = [pk[n] for n in ("ssd_dt_bias", "ssd_a_log", "ssd_d")]
    st_shape = (2, SSD_GROUPS, SSD_STATE, LANE)
    st_spec = pl.BlockSpec((nb,) + st_shape, lambda bi, j: (bi, 0, 0, 0, 0))
    y_shape = jax.ShapeDtypeStruct((b, l, SSD_WIDTH), F32)
    per_dir = lambda f, ft: [rows(SSD_WIDTH, f), rows(SSD_GROUPS * SSD_STATE, f),
                             cols(SSD_GROUPS * SSD_STATE, ft), rows(LANE, f)]
    return pl.pallas_call(
        functools.partial(_ssd_kernel, nc=nc),
        out_shape=[y_shape, y_shape, jax.ShapeDtypeStruct((b,) + st_shape, F32)],
        grid=(b // nb, nc),
        in_specs=per_dir(fwd, fwd_t) + per_dir(bwd, bwd_t) + [_layer_spec(w, layer) for w in small] + [st_spec],
        out_specs=[rows(SSD_WIDTH, fwd), rows(SSD_WIDTH, bwd), st_spec],
        scratch_shapes=[pltpu.VMEM((nb,) + st_shape, F32)],
        compiler_params=pltpu.CompilerParams(dimension_semantics=("arbitrary", "arbitrary"),
                                             vmem_limit_bytes=VMEM_LIMIT),
        name="ssd_scan",
    )(xs, cmo, bmt, dt, xs, cmo, bmt, dt, *small, init)


def _post_kernel(x_ref, ya_ref, ycm_ref, yf_ref, yb_ref, z_ref, mod_ref,
                 gssd_ref, gpm_ref, gpf_ref, gpo_ref, wout_ref, w1_ref, w2_ref, o_ref, *, tf):
    d = x_ref.shape[-1]
    f = w1_ref.shape[-1]
    gate1 = mod_ref[:, 2 * d:3 * d]
    shift2 = mod_ref[:, 3 * d:4 * d]
    scale2 = mod_ref[:, 4 * d:5 * d]
    gate2 = mod_ref[:, 5 * d:6 * d]
    wa = HEADS * VDIM
    tm = x_ref.shape[0]
    sub = min(tm, SUB_ROWS)

    def mixed(r0):
        rs_ = slice(r0, r0 + sub)
        y_ssd = _rms((yf_ref[rs_, :] + yb_ref[rs_, :]) * _silu(z_ref[rs_, :]), gssd_ref[...]).astype(BF16)
        y = (_dot(ya_ref[rs_, :], wout_ref[0:wa, :]) + _dot(ycm_ref[rs_, :], wout_ref[wa:wa + CM_WIDTH, :])
             + _dot(y_ssd, wout_ref[wa + CM_WIDTH:, :]))
        x1 = x_ref[rs_, :] + gate1 * _rms(y, gpm_ref[...])
        return x1, (_rms(x1, gpf_ref[...]) * (1.0 + scale2) + shift2).astype(BF16)

    nxt = mixed(0)
    for r0 in range(0, tm, sub):
        rs_ = slice(r0, r0 + sub)
        x1, h2 = nxt
        if r0 + sub < tm:
            nxt = mixed(r0 + sub)
        acc = jnp.zeros(x1.shape, F32)
        for c in range(f // tf):
            hc = jnp.maximum(_dot(h2, w1_ref[:, c * tf:(c + 1) * tf]), 0.0)
            acc = acc + _dot((hc * hc).astype(BF16), w2_ref[c * tf:(c + 1) * tf, :])
        o_ref[rs_, :] = x1 + gate2 * _rms(acc, gpo_ref[...])


def _post_call(x, ya, ycm, yf, yb, z, mods, layer, mod_row, pk, tm):
    b, l, d = x.shape
    row_spec = lambda w: pl.BlockSpec((None, tm, w), lambda bi, i: (bi, i, 0))
    weights = [pk[n] for n in ("ssd_norm_g", "g_post_mix", "g_pre_ff", "g_post_ff", "w_out", "w_ff1", "w_ff2")]
    return pl.pallas_call(
        functools.partial(_post_kernel, tf=1024),
        out_shape=jax.ShapeDtypeStruct((b, l, d), F32),
        grid=(b, l // tm),
        in_specs=[row_spec(d), row_spec(HEADS * VDIM), row_spec(CM_WIDTH), row_spec(SSD_WIDTH), row_spec(SSD_WIDTH),
                  row_spec(SSD_WIDTH), _mod_spec(mods, layer, mod_row)]
        + [_layer_spec(w, layer, single=True) for w in weights],
        out_specs=row_spec(d),
        compiler_params=pltpu.CompilerParams(dimension_semantics=("arbitrary", "arbitrary"),
                                             vmem_limit_bytes=VMEM_LIMIT),
        name="post_mix_ffn",
    )(x, ya, ycm, yf, yb, z, mods, *weights)


def _rope_tables(n_tokens):
    rows_n = n_tokens // GRID_W
    row = jnp.repeat(jnp.arange(rows_n, dtype=F32), GRID_W)
    col = jnp.tile(jnp.arange(GRID_W, dtype=F32), rows_n)
    axis_dim = ROPE // 2
    inv_freq = ROPE_BASE ** (-jnp.arange(0, axis_dim, 2, dtype=F32) / axis_dim)
    ang_r = row[:, None] * inv_freq
    ang_c = col[:, None] * inv_freq
    zeros = jnp.zeros((n_tokens, LANE - ROPE), F32)
    cos = jnp.concatenate([jnp.cos(ang_r), jnp.cos(ang_r), jnp.cos(ang_c), jnp.cos(ang_c), zeros], axis=-1)
    sin = jnp.concatenate([-jnp.sin(ang_r), jnp.sin(ang_r), -jnp.sin(ang_c), jnp.sin(ang_c), zeros], axis=-1)
    return cos, sin


def _rotate_half_partner():
    q = ROPE // 4
    return np.concatenate([np.arange(q, 2 * q), np.arange(0, q), np.arange(3 * q, 4 * q), np.arange(2 * q, 3 * q)])


def _pack_weights(p):
    depth = p["w_in"].shape[0]
    partner = _rotate_half_partner()
    w_in = p["w_in"]
    kr = w_in[..., 384:448]
    pad = jnp.zeros(w_in.shape[:-1] + (LANE - 2 * SSD_HEADS,), w_in.dtype)
    w_in_p = jnp.concatenate([w_in[..., :384], kr, kr[..., partner], w_in[..., 448:1984], w_in[..., 1984:], pad], axis=-1)
    w_uq = p["w_uq"].reshape(depth, Q_LORA, HEADS, NOPE + ROPE)
    rope_cols = w_uq[..., NOPE:]
    w_uq_p = jnp.concatenate([w_uq[..., :NOPE], rope_cols, rope_cols[..., partner]], axis=-1)
    row = lambda a: a.reshape(depth, 1, -1)
    pad_lane = lambda a: jnp.pad(a.reshape(depth, 1, -1), ((0, 0), (0, 0), (0, LANE - 2 * SSD_HEADS)))
    return {
        "g_pre_mix": row(p["g_pre_mix"]), "g_post_mix": row(p["g_post_mix"]),
        "g_pre_ff": row(p["g_pre_ff"]), "g_post_ff": row(p["g_post_ff"]),
        "w_in": w_in_p.astype(BF16),
        "g_q": row(p["g_q"]), "w_uq": w_uq_p.reshape(depth, Q_LORA, HEADS * QK_PAD).astype(BF16),
        "g_kv": row(p["g_kv"]), "w_ukv": p["w_ukv"].astype(BF16),
        "cm_norm_g": row(p["cm_norm_g"]),
        "cm_w": jnp.transpose(p["cm_w_s"], (0, 2, 1, 3)).reshape(depth, CHUNK, CM_GROUPS * CHUNK).astype(BF16),
        "cm_b": jnp.repeat(jnp.transpose(p["cm_b_s"], (0, 2, 1)), CM_GROUP_DIM, axis=-1),
        "ssd_conv_w": p["ssd_conv_w"], "ssd_conv_b": row(p["ssd_conv_b"]),
        "ssd_dt_bias": pad_lane(p["ssd_dt_bias"]), "ssd_a_log": pad_lane(p["ssd_a_log"]),
        "ssd_d": jnp.repeat(p["ssd_d"], SSD_WIDTH // SSD_HEADS, axis=-1),
        "ssd_norm_g": row(p["ssd_norm_g"]),
        "w_out": p["w_out"].astype(BF16), "w_ff1": p["w_ff1"].astype(BF16), "w_ff2": p["w_ff2"].astype(BF16),
    }


def _row_tile(l, want):
    return want if l % want == 0 else l


def kernel(x, c, ctx, c_ctx, w_ada, b_ada, g_pre_mix, g_post_mix, g_pre_ff, g_post_ff, w_in, g_q, w_uq, g_kv, w_ukv, cm_norm_g, cm_w_s, cm_b_s, ssd_conv_w, ssd_conv_b, ssd_dt_bias, ssd_a_log, ssd_d, ssd_norm_g, w_out, w_ff1, w_ff2):
    b, n_lat, d = x.shape
    n_ctx = ctx.shape[1]
    depth = w_ada.shape[0]
    assert b + 1 <= MOD_ROWS and n_lat % CHUNK == 0 and n_ctx % CHUNK == 0 and n_lat % GRID_W == 0

    pk = _pack_weights(dict(
        g_pre_mix=g_pre_mix, g_post_mix=g_post_mix, g_pre_ff=g_pre_ff, g_post_ff=g_post_ff, w_in=w_in, g_q=g_q,
        w_uq=w_uq, g_kv=g_kv, w_ukv=w_ukv, cm_norm_g=cm_norm_g, cm_w_s=cm_w_s, cm_b_s=cm_b_s,
        ssd_conv_w=ssd_conv_w, ssd_conv_b=ssd_conv_b, ssd_dt_bias=ssd_dt_bias, ssd_a_log=ssd_a_log, ssd_d=ssd_d,
        ssd_norm_g=ssd_norm_g, w_out=w_out, w_ff1=w_ff1, w_ff2=w_ff2))

    cc = jnp.concatenate([c, c_ctx[None, :], jnp.zeros((MOD_ROWS - b - 1, d), c.dtype)], axis=0)
    mods = _mod_call(cc, w_ada, b_ada).reshape(depth, MOD_ROWS, 1, 6 * d)

    cos_l, sin_l = _rope_tables(n_lat)
    cos_c = jnp.concatenate([jnp.ones((n_ctx, ROPE), F32), jnp.zeros((n_ctx, LANE - ROPE), F32)], axis=-1)
    sin_c = jnp.zeros((n_ctx, LANE), F32)

    tm_l = _row_tile(n_lat, 1024)
    tm_c = _row_tile(n_ctx, 256)
    tq_l = _row_tile(n_lat, 1024)
    tk = 512
    zero_states = jnp.zeros((b, 2, SSD_GROUPS, SSD_STATE, LANE), F32)

    xc = ctx
    for layer in range(depth):
        last = layer == depth - 1

        cq, ck, cv, cycm, cz, cxs, ccm, cbmt, cdt = _pre_call(xc, mods, layer, b, pk, cos_c, sin_c, tm_c)
        cyf, cyb, ctx_states = _ssd_call(cxs, ccm, cbmt, cdt, pk, layer, zero_states)
        if not last:
            cya = _attn_call(cq, [(ck, cv)], tm_c, tk)
            xc = _post_call(xc, cya, cycm, cyf, cyb, cz, mods, layer, b, pk, tm_c)

        q, k, v, ycm, z, xs, cmo, bmt, dt = _pre_call(x, mods, layer, None, pk, cos_l, sin_l, tm_l)
        yf, yb, _ = _ssd_call(xs, cmo, bmt, dt, pk, layer, ctx_states)
        ya = _attn_call(q, [(k, v), (ck, cv)], tq_l, tk)
        x = _post_call(x, ya, ycm, yf, yb, z, mods, layer, None, pk, tm_l)
    return x
```

```python
import functools
import math

import numpy as np
import jax
import jax.numpy as jnp
from jax import lax
from jax.experimental import pallas as pl
from jax.experimental.pallas import tpu as pltpu

F32 = jnp.float32
BF16 = jnp.bfloat16

EPS = 1e-6
GRID_W = 64
ROPE_BASE = 10000.0

HEADS = 4
Q_LORA = 256
KV_LORA = 128
NOPE = 128
ROPE = 64
VDIM = 128
QK_PAD = 256
ATTN_SCALE = (NOPE + ROPE) ** -0.5
QSCALE = ATTN_SCALE * math.log2(math.e)

CM_GROUPS = 4
CM_WIDTH = 256
CM_GROUP_DIM = CM_WIDTH // CM_GROUPS
CHUNK = 128

SSD_WIDTH = 256
SSD_HEADS = 4
SSD_GROUPS = 2
SSD_STATE = 128
SSD_CONV_DIM = SSD_WIDTH + 2 * SSD_GROUPS * SSD_STATE

LANE = 128
SUBLANE = 8
HALO = 16
ONES_ROWS = 16
MOD_ROWS = 16
SUB_ROWS = 256

P_Q, P_KV, P_KR, P_CM, P_Z, P_XBC, P_DT, P_END = 0, 256, 384, 512, 1024, 1280, 2048, 2176

VMEM_LIMIT = 56 * 1024 * 1024


def _rms(x, g):
    ms = jnp.mean(x * x, axis=-1, keepdims=True)
    return x * lax.rsqrt(ms + EPS) * g


def _sigmoid(x):
    return 1.0 / (1.0 + jnp.exp(-x))


def _silu(x):
    return x * _sigmoid(x)


def _dot(a, b):
    return jnp.dot(a, b, preferred_element_type=F32)


def _dot_nt(a, b):
    return lax.dot_general(a, b, (((1,), (1,)), ((), ())), preferred_element_type=F32)


def _layer_spec(w, layer, single=False):
    idx = (layer,) + (0,) * (w.ndim - 1)
    mode = dict(pipeline_mode=pl.Buffered(1)) if single else {}
    return pl.BlockSpec((None,) + w.shape[1:], lambda *_: idx, **mode)


def _mod_spec(mods, layer, mod_row):
    if mod_row is None:
        return pl.BlockSpec((None, None, 1, mods.shape[-1]), lambda bi, i: (layer, bi, 0, 0))
    return pl.BlockSpec((None, None, 1, mods.shape[-1]), lambda bi, i: (layer, mod_row, 0, 0))


def _mod_kernel(c_ref, w_ref, b_ref, o_ref):
    s = _silu(c_ref[...]).astype(BF16)
    o_ref[...] = _dot(s, w_ref[...].astype(BF16)) + b_ref[...]


def _mod_call(cc, w_ada, b_ada):
    depth, d, n = w_ada.shape
    tn = 1536 if n % 1536 == 0 else 512
    return pl.pallas_call(
        _mod_kernel,
        out_shape=jax.ShapeDtypeStruct((depth, MOD_ROWS, n), F32),
        grid=(depth, n // tn),
        in_specs=[
            pl.BlockSpec((MOD_ROWS, d), lambda l, j: (0, 0)),
            pl.BlockSpec((None, d, tn), lambda l, j: (l, 0, j)),
            pl.BlockSpec((None, 1, tn), lambda l, j: (l, 0, j)),
        ],
        out_specs=pl.BlockSpec((None, MOD_ROWS, tn), lambda l, j: (l, 0, j)),
        compiler_params=pltpu.CompilerParams(dimension_semantics=("arbitrary", "arbitrary"),
                                             vmem_limit_bytes=VMEM_LIMIT),
        name="adaln_rows",
    )(cc, w_ada, b_ada.reshape(depth, 1, n))


def _pre_kernel(x_ref, xp_ref, xn_ref, mod_ref, gpre_ref, win_ref, gq_ref, wuq_ref, gkv_ref, wukv_ref,
                cos_ref, sin_ref, cmg_ref, cmw_ref, cmb_ref, convw_ref, convb_ref,
                q_ref, k_ref, vt_ref, ycm_ref, z_ref, xs_ref, cmo_ref, bmt_ref, dt_ref, h_scr):
    d = x_ref.shape[-1]
    tm = x_ref.shape[0]
    i = pl.program_id(1)
    last_tile = pl.num_programs(1) - 1
    shift = mod_ref[:, 0:d]
    scale = mod_ref[:, d:2 * d]
    lane_group = lax.broadcasted_iota(jnp.int32, (CHUNK, CM_WIDTH), 1) // CM_GROUP_DIM
    sub = min(tm, SUB_ROWS)
    row_id = lax.broadcasted_iota(jnp.int32, (sub, SSD_CONV_DIM), 0)
    cw = convw_ref[...]

    def normed(xv):
        return (_rms(xv, gpre_ref[...]) * (1.0 + scale) + shift).astype(BF16)

    h_scr[0:HALO, :] = normed(xp_ref[...])
    h_scr[HALO + tm:2 * HALO + tm, :] = normed(xn_ref[...])
    h_scr[HALO:HALO + sub, :] = normed(x_ref[0:sub, :])
    for r0 in range(0, tm, sub):
        rs_ = slice(r0, r0 + sub)
        if r0 + sub < tm:
            h_scr[HALO + r0 + sub:HALO + r0 + 2 * sub, :] = normed(x_ref[r0 + sub:r0 + 2 * sub, :])
        h = h_scr[HALO + r0:HALO + r0 + sub, :]

        def proj(lo, hi):
            return _dot(h, win_ref[:, lo:hi])

        cos = cos_ref[rs_, :]
        sin = sin_ref[rs_, :]

        def rotate(rs):
            return rs * cos + pltpu.roll(rs, ROPE, 1) * sin

        q_lo = proj(P_Q, P_KV)
        kvkr = proj(P_KV, P_CM)
        xcm = proj(P_CM, P_Z)
        xbc = _dot(h_scr[r0:r0 + sub + 2 * HALO, :], win_ref[:, P_XBC:P_DT])
        z_ref[rs_, :] = proj(P_Z, P_XBC)
        dt_ref[rs_, :] = proj(P_DT, P_END)

        qn = _rms(q_lo, gq_ref[...]).astype(BF16)
        qf = _dot(qn, wuq_ref[...])
        kvn = _rms(kvkr[:, :KV_LORA], gkv_ref[...]).astype(BF16)
        kvf = _dot(kvn, wukv_ref[...])
        for hh in range(HEADS):
            lo = hh * QK_PAD
            q_ref[rs_, lo:lo + NOPE] = qf[:, lo:lo + NOPE].astype(BF16)
            q_ref[rs_, lo + NOPE:lo + QK_PAD] = rotate(qf[:, lo + NOPE:lo + QK_PAD]).astype(BF16)
        krot = rotate(kvkr[:, KV_LORA:]).astype(BF16)
        for hh in range(HEADS):
            lo = hh * QK_PAD
            k_ref[rs_, lo:lo + NOPE] = kvf[:, hh * (NOPE + VDIM):hh * (NOPE + VDIM) + NOPE].astype(BF16)
            k_ref[rs_, lo + NOPE:lo + QK_PAD] = krot
            vt_ref[hh * VDIM:(hh + 1) * VDIM, rs_] = (
                kvf[:, hh * (NOPE + VDIM) + NOPE:(hh + 1) * (NOPE + VDIM)].T.astype(BF16))

        n_ext = sub + 2 * HALO
        x_prev = pltpu.roll(xbc, 1, 0)[HALO:HALO + sub, :]
        x_next = pltpu.roll(xbc, n_ext - 1, 0)[HALO:HALO + sub, :]
        if r0 == 0:
            x_prev = jnp.where((row_id == 0) & (i == 0), 0.0, x_prev)
        if r0 + sub == tm:
            x_next = jnp.where((row_id == sub - 1) & (i == last_tile), 0.0, x_next)
        xa = _silu(cw[0:1, :] * x_prev + cw[1:2, :] * xbc[HALO:HALO + sub, :] + cw[2:3, :] * x_next + convb_ref[...])
        xs_ref[rs_, :] = xa[:, :SSD_WIDTH]
        cmo_ref[rs_, :] = xa[:, SSD_WIDTH + SSD_GROUPS * SSD_STATE:].astype(BF16)
        for g in range(SSD_GROUPS):
            for c in range(sub // CHUNK):
                blk = xa[c * CHUNK:(c + 1) * CHUNK, SSD_WIDTH + g * SSD_STATE:SSD_WIDTH + (g + 1) * SSD_STATE]
                bmt_ref[g * SSD_STATE:(g + 1) * SSD_STATE, r0 + c * CHUNK:r0 + (c + 1) * CHUNK] = blk.T.astype(BF16)

        gl = xcm * (0.5 * (1.0 + jnp.tanh(math.sqrt(2.0 / math.pi) * (xcm + 0.044715 * (xcm * xcm * xcm)))))
        u = gl[:, :CM_WIDTH]
        vv = gl[:, CM_WIDTH:]
        mu = jnp.mean(vv, axis=-1, keepdims=True)
        vc = vv - mu
        var = jnp.mean(vc * vc, axis=-1, keepdims=True)
        vn = vc * lax.rsqrt(var + EPS) * cmg_ref[...]
        for c in range(sub // CHUNK):
            rows = slice(c * CHUNK, (c + 1) * CHUNK)
            vchunk = vn[rows, :]
            vblk = jnp.concatenate(
                [jnp.where(lane_group == g, vchunk, 0.0).astype(BF16) for g in range(CM_GROUPS)], axis=0)
            mixed = _dot(cmw_ref[...], vblk) + cmb_ref[...]
            ycm_ref[r0 + c * CHUNK:r0 + (c + 1) * CHUNK, :] = (u[rows, :] * mixed).astype(BF16)


def _pre_call(x, mods, layer, mod_row, pk, cos, sin, tm):
    b, l, d = x.shape
    nt = l // tm
    per = tm // HALO
    last_halo = l // HALO - 1
    row_spec = lambda w: pl.BlockSpec((None, tm, w), lambda bi, i: (bi, i, 0))
    col_spec = lambda h: pl.BlockSpec((None, h, tm), lambda bi, i: (bi, 0, i))
    halo_prev = pl.BlockSpec((None, HALO, d), lambda bi, i: (bi, jnp.maximum(i * per - 1, 0), 0))
    halo_next = pl.BlockSpec((None, HALO, d), lambda bi, i: (bi, jnp.minimum((i + 1) * per, last_halo), 0))
    tab_spec = pl.BlockSpec((tm, LANE), lambda bi, i: (i, 0))
    weights = [pk[n] for n in ("g_pre_mix", "w_in", "g_q", "w_uq", "g_kv", "w_ukv")]
    tail_weights = [pk[n] for n in ("cm_norm_g", "cm_w", "cm_b", "ssd_conv_w", "ssd_conv_b")]
    outs = [((l, HEADS * QK_PAD), BF16, row_spec(HEADS * QK_PAD)), ((l, HEADS * QK_PAD), BF16, row_spec(HEADS * QK_PAD)),
            ((HEADS * VDIM, l), BF16, col_spec(HEADS * VDIM)), ((l, CM_WIDTH), BF16, row_spec(CM_WIDTH)),
            ((l, SSD_WIDTH), F32, row_spec(SSD_WIDTH)), ((l, SSD_WIDTH), F32, row_spec(SSD_WIDTH)),
            ((l, SSD_GROUPS * SSD_STATE), BF16, row_spec(SSD_GROUPS * SSD_STATE)),
            ((SSD_GROUPS * SSD_STATE, l), BF16, col_spec(SSD_GROUPS * SSD_STATE)), ((l, LANE), F32, row_spec(LANE))]
    return pl.pallas_call(
        _pre_kernel,
        out_shape=[jax.ShapeDtypeStruct((b,) + s, dt) for s, dt, _ in outs],
        grid=(b, nt),
        in_specs=[row_spec(d), halo_prev, halo_next, _mod_spec(mods, layer, mod_row)]
        + [_layer_spec(w, layer) for w in weights] + [tab_spec, tab_spec]
        + [_layer_spec(w, layer) for w in tail_weights],
        out_specs=[sp for _, _, sp in outs],
        scratch_shapes=[pltpu.VMEM((tm + 2 * HALO, d), BF16)],
        compiler_params=pltpu.CompilerParams(dimension_semantics=("arbitrary", "arbitrary"),
                                             vmem_limit_bytes=VMEM_LIMIT),
        name="pre_mix",
    )(x, x, x, mods, *weights, cos, sin, *tail_weights)


def _attn_kernel(q_ref, *refs, tk):
    o_ref, s_scr = refs[-2:]
    kv_refs = refs[:-2]
    tq = q_ref.shape[0]
    q = q_ref[...]
    tiles = []
    for seg in range(len(kv_refs) // 2):
        k_ref, vt_ref = kv_refs[2 * seg], kv_refs[2 * seg + 1]
        n = k_ref.shape[0]
        step = min(tk, n)
        tiles += [(k_ref, vt_ref, start, step) for start in range(0, n, step)]

    def scores_into(tile, slot):
        k_ref, _, start, step = tile
        s = _dot_nt(k_ref[start:start + step, :], q)
        s_scr[slot, 0:step, :] = s
        return jnp.max(s, axis=0, keepdims=True)

    m = jnp.full((1, tq), -1e30, F32)
    acc = jnp.zeros((VDIM + ONES_ROWS, tq), F32)
    mx = scores_into(tiles[0], 0)
    for t, (_, vt_ref, start, step) in enumerate(tiles):
        slot = t % 2
        mx_next = scores_into(tiles[t + 1], 1 - slot) if t + 1 < len(tiles) else None
        m_new = jnp.maximum(m, mx)
        alpha = jnp.exp2(m - m_new)
        p = jnp.exp2(s_scr[slot, 0:step, :] - m_new).astype(BF16)
        vt_ext = jnp.concatenate([vt_ref[:, start:start + step], jnp.ones((ONES_ROWS, step), BF16)], axis=0)
        acc = alpha * acc + _dot(vt_ext, p)
        m = m_new
        mx = mx_next
    o_ref[...] = (acc[0:VDIM, :] * (1.0 / acc[VDIM:VDIM + 1, :])).T.astype(BF16)


def _attn_call(q, kvs, tq, tk):
    b, lq, _ = q.shape
    in_specs = [pl.BlockSpec((None, tq, QK_PAD), lambda bi, hi, i: (bi, i, hi))]
    args = [q]
    for k, vt in kvs:
        lk = k.shape[1]
        in_specs.append(pl.BlockSpec((None, lk, QK_PAD), lambda bi, hi, i: (bi, 0, hi)))
        in_specs.append(pl.BlockSpec((None, VDIM, lk), lambda bi, hi, i: (bi, hi, 0)))
        args += [k, vt]
    return pl.pallas_call(
        functools.partial(_attn_kernel, tk=tk),
        out_shape=jax.ShapeDtypeStruct((b, lq, HEADS * VDIM), BF16),
        grid=(b, HEADS, lq // tq),
        in_specs=in_specs,
        out_specs=pl.BlockSpec((None, tq, VDIM), lambda bi, hi, i: (bi, i, hi)),
        scratch_shapes=[pltpu.VMEM((2, tk, tq), F32)],
        compiler_params=pltpu.CompilerParams(dimension_semantics=("arbitrary", "arbitrary", "arbitrary"),
                                             vmem_limit_bytes=VMEM_LIMIT),
        name="mla_attention",
    )(*args)


def _ssd_kernel(xs_f, cm_f, bmt_f, dt_f, xs_b, cm_b_, bmt_b, dt_b,
                dtbias_ref, alog_ref, drep_ref, init_ref,
                yf_ref, yb_ref, fin_ref, st_ref, *, nc):
    j = pl.program_id(1)

    @pl.when(j == 0)
    def _():
        st_ref[...] = init_ref[...]

    nb = xs_f.shape[0]
    a_row = -jnp.exp(alog_ref[...])
    ri = lax.broadcasted_iota(jnp.int32, (CHUNK, CHUNK), 0)
    ci = lax.broadcasted_iota(jnp.int32, (CHUNK, CHUNK), 1)
    low_half = ci < (LANE // 2)
    causal = ((ri >= ci), (ri <= ci))
    tri = tuple(jnp.where(cz, 1.0, 0.0).astype(BF16) for cz in causal)
    streams = ((xs_f, cm_f, bmt_f, dt_f, yf_ref), (xs_b, cm_b_, bmt_b, dt_b, yb_ref))
    chains = [(n, d) for n in range(nb) for d in range(2)]
    groups = [(n, d, g) for (n, d) in chains for g in range(SSD_GROUPS)]

    dtv, a_parts = {}, {}
    for n, d in chains:
        dt_in = streams[d][3][n] + dtbias_ref[...]
        dtv[n, d] = jnp.maximum(dt_in, 0.0) + jnp.log1p(jnp.exp(-jnp.abs(dt_in)))
        a = dtv[n, d] * a_row
        a_hi = a.astype(BF16)
        r1 = a - a_hi.astype(F32)
        a_mid = r1.astype(BF16)
        a_parts[n, d] = (a_hi, a_mid, (r1 - a_mid.astype(F32)).astype(BF16))

    cum = {}
    for n, d in chains:
        a_hi, a_mid, a_lo = a_parts[n, d]
        cum[n, d] = _dot(tri[d], a_hi) + _dot(tri[d], a_mid) + _dot(tri[d], a_lo)

    cmb, gmat, y_off_raw, st_old = {}, {}, {}, {}
    for n, d, g in groups:
        gl = slice(g * LANE, (g + 1) * LANE)
        cmb[n, d, g] = streams[d][1][n, :, gl]
        gmat[n, d, g] = _dot(cmb[n, d, g], streams[d][2][n, g * SSD_STATE:(g + 1) * SSD_STATE, :])
        st_old[n, d, g] = st_ref[n, d, g]
        y_off_raw[n, d, g] = _dot(cmb[n, d, g], st_old[n, d, g].astype(BF16))

    cum_t, tot = {}, {}
    for n, d in chains:
        cum_t[n, d] = cum[n, d].T
        tot[n, d] = cum[n, d][CHUNK - 1:CHUNK, :] if d == 0 else cum[n, d][0:1, :]
    colx, totx, xdt_b, xdec_b, scores = {}, {}, {}, {}, {}
    for n, d, g in groups:
        i0 = d * SSD_HEADS + 2 * g
        i1 = i0 + 1
        gl = slice(g * LANE, (g + 1) * LANE)
        c_nd = cum[n, d]
        colx[n, d, g] = jnp.where(low_half, c_nd[:, i0:i0 + 1], c_nd[:, i1:i1 + 1])
        dtx = jnp.where(low_half, dtv[n, d][:, i0:i0 + 1], dtv[n, d][:, i1:i1 + 1])
        totx[n, d, g] = jnp.where(low_half[0:1, :], tot[n, d][:, i0:i0 + 1], tot[n, d][:, i1:i1 + 1])
        xdt = streams[d][0][n, :, gl] * dtx
        xdt_b[n, d, g] = xdt.astype(BF16)
        xdec_b[n, d, g] = (xdt * jnp.exp(totx[n, d, g] - colx[n, d, g])).astype(BF16)
        for hh, ii in enumerate((i0, i1)):
            diff = c_nd[:, ii:ii + 1] - cum_t[n, d][ii:ii + 1, :]
            lmat = jnp.exp(jnp.where(causal[d], diff, -1e30))
            scores[n, d, g, hh] = (gmat[n, d, g] * lmat).astype(BF16)

    ys, st_add = {}, {}
    for n, d, g in groups:
        ys[n, d, g] = [_dot(scores[n, d, g, hh], xdt_b[n, d, g]) for hh in range(2)]
        st_add[n, d, g] = _dot(streams[d][2][n, g * SSD_STATE:(g + 1) * SSD_STATE, :], xdec_b[n, d, g])

    for n, d, g in groups:
        gl = slice(g * LANE, (g + 1) * LANE)
        y = jnp.where(low_half, ys[n, d, g][0], ys[n, d, g][1]) + y_off_raw[n, d, g] * jnp.exp(colx[n, d, g])
        if d == 0:
            y = y + (drep_ref[0:1, gl] + drep_ref[1:2, gl]) * streams[d][0][n, :, gl]
        streams[d][4][n, :, gl] = y
        st_ref[n, d, g] = jnp.exp(totx[n, d, g]) * st_old[n, d, g] + st_add[n, d, g]

    @pl.when(j == nc - 1)
    def _():
        fin_ref[...] = st_ref[...]


def _ssd_call(xs, cmo, bmt, dt, pk, layer, init, nb_want=8):
    b, l, _ = xs.shape
    nc = l // CHUNK
    fwd = lambda bi, j: (bi, j, 0)
    bwd = lambda bi, j: (bi, nc - 1 - j, 0)
    fwd_t = lambda bi, j: (bi, 0, j)
    bwd_t = lambda bi, j: (bi, 0, nc - 1 - j)
    nb = nb_want if b % nb_want == 0 else 1
    rows = lambda w, f: pl.BlockSpec((nb, CHUNK, w), f)
    cols = lambda h, f: pl.BlockSpec((nb, h, CHUNK), f)
    small = [pk[n] for n in ("ssd_dt_bias", "ssd_a_log", "ssd_d")]
    st_shape = (2, SSD_GROUPS, SSD_STATE, LANE)
    st_spec = pl.BlockSpec((nb,) + st_shape, lambda bi, j: (bi, 0, 0, 0, 0))
    y_shape = jax.ShapeDtypeStruct((b, l, SSD_WIDTH), F32)
    per_dir = lambda f, ft: [rows(SSD_WIDTH, f), rows(SSD_GROUPS * SSD_STATE, f),
                             cols(SSD_GROUPS * SSD_STATE, ft), rows(LANE, f)]
    return pl.pallas_call(
        functools.partial(_ssd_kernel, nc=nc),
        out_shape=[y_shape, y_shape, jax.ShapeDtypeStruct((b,) + st_shape, F32)],
        grid=(b // nb, nc),
        in_specs=per_dir(fwd, fwd_t) + per_dir(bwd, bwd_t) + [_layer_spec(w, layer) for w in small] + [st_spec],
        out_specs=[rows(SSD_WIDTH, fwd), rows(SSD_WIDTH, bwd), st_spec],
        scratch_shapes=[pltpu.VMEM((nb,) + st_shape, F32)],
        compiler_params=pltpu.CompilerParams(dimension_semantics=("arbitrary", "arbitrary"),
                                             vmem_limit_bytes=VMEM_LIMIT),
        name="ssd_scan",
    )(xs, cmo, bmt, dt, xs, cmo, bmt, dt, *small, init)


def _post_kernel(x_ref, ya_ref, ycm_ref, yf_ref, yb_ref, z_ref, mod_ref,
                 gssd_ref, gpm_ref, gpf_ref, gpo_ref, wout_ref, w1_ref, w2_ref, o_ref, *, tf):
    d = x_ref.shape[-1]
    f = w1_ref.shape[-1]
    gate1 = mod_ref[:, 2 * d:3 * d]
    shift2 = mod_ref[:, 3 * d:4 * d]
    scale2 = mod_ref[:, 4 * d:5 * d]
    gate2 = mod_ref[:, 5 * d:6 * d]
    wa = HEADS * VDIM
    tm = x_ref.shape[0]
    sub = min(tm, SUB_ROWS)

    def mixed(r0):
        rs_ = slice(r0, r0 + sub)
        y_ssd = _rms((yf_ref[rs_, :] + yb_ref[rs_, :]) * _silu(z_ref[rs_, :]), gssd_ref[...]).astype(BF16)
        y = (_dot(ya_ref[rs_, :], wout_ref[0:wa, :]) + _dot(ycm_ref[rs_, :], wout_ref[wa:wa + CM_WIDTH, :])
             + _dot(y_ssd, wout_ref[wa + CM_WIDTH:, :]))
        x1 = x_ref[rs_, :] + gate1 * _rms(y, gpm_ref[...])
        return x1, (_rms(x1, gpf_ref[...]) * (1.0 + scale2) + shift2).astype(BF16)

    nxt = mixed(0)
    for r0 in range(0, tm, sub):
        rs_ = slice(r0, r0 + sub)
        x1, h2 = nxt
        if r0 + sub < tm:
            nxt = mixed(r0 + sub)
        acc = jnp.zeros(x1.shape, F32)
        for c in range(f // tf):
            hc = jnp.maximum(_dot(h2, w1_ref[:, c * tf:(c + 1) * tf]), 0.0)
            acc = acc + _dot((hc * hc).astype(BF16), w2_ref[c * tf:(c + 1) * tf, :])
        o_ref[rs_, :] = x1 + gate2 * _rms(acc, gpo_ref[...])


def _post_call(x, ya, ycm, yf, yb, z, mods, layer, mod_row, pk, tm):
    b, l, d = x.shape
    row_spec = lambda w: pl.BlockSpec((None, tm, w), lambda bi, i: (bi, i, 0))
    weights = [pk[n] for n in ("ssd_norm_g", "g_post_mix", "g_pre_ff", "g_post_ff", "w_out", "w_ff1", "w_ff2")]
    return pl.pallas_call(
        functools.partial(_post_kernel, tf=1024),
        out_shape=jax.ShapeDtypeStruct((b, l, d), F32),
        grid=(b, l // tm),
        in_specs=[row_spec(d), row_spec(HEADS * VDIM), row_spec(CM_WIDTH), row_spec(SSD_WIDTH), row_spec(SSD_WIDTH),
                  row_spec(SSD_WIDTH), _mod_spec(mods, layer, mod_row)]
        + [_layer_spec(w, layer, single=True) for w in weights],
        out_specs=row_spec(d),
        compiler_params=pltpu.CompilerParams(dimension_semantics=("arbitrary", "arbitrary"),
                                             vmem_limit_bytes=VMEM_LIMIT),
        name="post_mix_ffn",
    )(x, ya, ycm, yf, yb, z, mods, *weights)


def _rope_tables(n_tokens):
    rows_n = n_tokens // GRID_W
    row = jnp.repeat(jnp.arange(rows_n, dtype=F32), GRID_W)
    col = jnp.tile(jnp.arange(GRID_W, dtype=F32), rows_n)
    axis_dim = ROPE // 2
    inv_freq = ROPE_BASE ** (-jnp.arange(0, axis_dim, 2, dtype=F32) / axis_dim)
    ang_r = row[:, None] * inv_freq
    ang_c = col[:, None] * inv_freq
    zeros = jnp.zeros((n_tokens, LANE - ROPE), F32)
    cos = jnp.concatenate([jnp.cos(ang_r), jnp.cos(ang_r), jnp.cos(ang_c), jnp.cos(ang_c), zeros], axis=-1)
    sin = jnp.concatenate([-jnp.sin(ang_r), jnp.sin(ang_r), -jnp.sin(ang_c), jnp.sin(ang_c), zeros], axis=-1)
    return cos, sin


def _rotate_half_partner():
    q = ROPE // 4
    return np.concatenate([np.arange(q, 2 * q), np.arange(0, q), np.arange(3 * q, 4 * q), np.arange(2 * q, 3 * q)])


def _pack_weights(p):
    depth = p["w_in"].shape[0]
    partner = _rotate_half_partner()
    w_in = p["w_in"]
    kr = w_in[..., 384:448]
    pad = jnp.zeros(w_in.shape[:-1] + (LANE - 2 * SSD_HEADS,), w_in.dtype)
    w_in_p = jnp.concatenate([w_in[..., :384], kr, kr[..., partner], w_in[..., 448:1984], w_in[..., 1984:], pad], axis=-1)
    w_uq = p["w_uq"].reshape(depth, Q_LORA, HEADS, NOPE + ROPE)
    rope_cols = w_uq[..., NOPE:]
    w_uq_p = jnp.concatenate([w_uq[..., :NOPE], rope_cols, rope_cols[..., partner]], axis=-1)
    row = lambda a: a.reshape(depth, 1, -1)
    pad_lane = lambda a: jnp.pad(a.reshape(depth, 1, -1), ((0, 0), (0, 0), (0, LANE - 2 * SSD_HEADS)))
    return {
        "g_pre_mix": row(p["g_pre_mix"]), "g_post_mix": row(p["g_post_mix"]),
        "g_pre_ff": row(p["g_pre_ff"]), "g_post_ff": row(p["g_post_ff"]),
        "w_in": w_in_p.astype(BF16),
        "g_q": row(p["g_q"]) * QSCALE, "w_uq": w_uq_p.reshape(depth, Q_LORA, HEADS * QK_PAD).astype(BF16),
        "g_kv": row(p["g_kv"]), "w_ukv": p["w_ukv"].astype(BF16),
        "cm_norm_g": row(p["cm_norm_g"]),
        "cm_w": jnp.transpose(p["cm_w_s"], (0, 2, 1, 3)).reshape(depth, CHUNK, CM_GROUPS * CHUNK).astype(BF16),
        "cm_b": jnp.repeat(jnp.transpose(p["cm_b_s"], (0, 2, 1)), CM_GROUP_DIM, axis=-1),
        "ssd_conv_w": p["ssd_conv_w"], "ssd_conv_b": row(p["ssd_conv_b"]),
        "ssd_dt_bias": pad_lane(p["ssd_dt_bias"]), "ssd_a_log": pad_lane(p["ssd_a_log"]),
        "ssd_d": jnp.repeat(p["ssd_d"], SSD_WIDTH // SSD_HEADS, axis=-1),
        "ssd_norm_g": row(p["ssd_norm_g"]),
        "w_out": p["w_out"].astype(BF16), "w_ff1": p["w_ff1"].astype(BF16), "w_ff2": p["w_ff2"].astype(BF16),
    }


def _row_tile(l, want):
    return want if l % want == 0 else l


def kernel(x, c, ctx, c_ctx, w_ada, b_ada, g_pre_mix, g_post_mix, g_pre_ff, g_post_ff, w_in, g_q, w_uq, g_kv, w_ukv, cm_norm_g, cm_w_s, cm_b_s, ssd_conv_w, ssd_conv_b, ssd_dt_bias, ssd_a_log, ssd_d, ssd_norm_g, w_out, w_ff1, w_ff2):
    b, n_lat, d = x.shape
    n_ctx = ctx.shape[1]
    depth = w_ada.shape[0]
    assert b + 1 <= MOD_ROWS and n_lat % CHUNK == 0 and n_ctx % CHUNK == 0 and n_lat % GRID_W == 0

    pk = _pack_weights(dict(
        g_pre_mix=g_pre_mix, g_post_mix=g_post_mix, g_pre_ff=g_pre_ff, g_post_ff=g_post_ff, w_in=w_in, g_q=g_q,
        w_uq=w_uq, g_kv=g_kv, w_ukv=w_ukv, cm_norm_g=cm_norm_g, cm_w_s=cm_w_s, cm_b_s=cm_b_s,
        ssd_conv_w=ssd_conv_w, ssd_conv_b=ssd_conv_b, ssd_dt_bias=ssd_dt_bias, ssd_a_log=ssd_a_log, ssd_d=ssd_d,
        ssd_norm_g=ssd_norm_g, w_out=w_out, w_ff1=w_ff1, w_ff2=w_ff2))

    cc = jnp.concatenate([c, c_ctx[None, :], jnp.zeros((MOD_ROWS - b - 1, d), c.dtype)], axis=0)
    mods = _mod_call(cc, w_ada, b_ada).reshape(depth, MOD_ROWS, 1, 6 * d)

    cos_l, sin_l = _rope_tables(n_lat)
    cos_c = jnp.concatenate([jnp.ones((n_ctx, ROPE), F32), jnp.zeros((n_ctx, LANE - ROPE), F32)], axis=-1)
    sin_c = jnp.zeros((n_ctx, LANE), F32)

    tm_l = _row_tile(n_lat, 1024)
    tm_c = _row_tile(n_ctx, 256)
    tq_l = _row_tile(n_lat, 1024)
    tk = 512
    zero_states = jnp.zeros((b, 2, SSD_GROUPS, SSD_STATE, LANE), F32)

    xc = ctx
    for layer in range(depth):
        last = layer == depth - 1

        cq, ck, cv, cycm, cz, cxs, ccm, cbmt, cdt = _pre_call(xc, mods, layer, b, pk, cos_c, sin_c, tm_c)
        cyf, cyb, ctx_states = _ssd_call(cxs, ccm, cbmt, cdt, pk, layer, zero_states)
        if not last:
            cya = _attn_call(cq, [(ck, cv)], tm_c, tk)
            xc = _post_call(xc, cya, cycm, cyf, cyb, cz, mods, layer, b, pk, tm_c)

        q, k, v, ycm, z, xs, cmo, bmt, dt = _pre_call(x, mods, layer, None, pk, cos_l, sin_l, tm_l)
        yf, yb, _ = _ssd_call(xs, cmo, bmt, dt, pk, layer, ctx_states)
        ya = _attn_call(q, [(k, v), (ck, cv)], tq_l, tk)
        x = _post_call(x, ya, ycm, yf, yb, z, mods, layer, None, pk, tm_l)
    return x
```

```python
import functools
import math

import numpy as np
import jax
import jax.numpy as jnp
from jax import lax
from jax.experimental import pallas as pl
from jax.experimental.pallas import tpu as pltpu

F32 = jnp.float32
BF16 = jnp.bfloat16

EPS = 1e-6
GRID_W = 64
ROPE_BASE = 10000.0

HEADS = 4
Q_LORA = 256
KV_LORA = 128
NOPE = 128
ROPE = 64
VDIM = 128
QK_PAD = 256
ATTN_SCALE = (NOPE + ROPE) ** -0.5
QSCALE = ATTN_SCALE * math.log2(math.e)

CM_GROUPS = 4
CM_WIDTH = 256
CM_GROUP_DIM = CM_WIDTH // CM_GROUPS
CHUNK = 128

SSD_WIDTH = 256
SSD_HEADS = 4
SSD_GROUPS = 2
SSD_STATE = 128
SSD_CONV_DIM = SSD_WIDTH + 2 * SSD_GROUPS * SSD_STATE

LANE = 128
HALO = 16
ONES_ROWS = 16
MOD_ROWS = 16
SUB_ROWS = 256

P_Q, P_KV, P_KR, P_CM, P_Z, P_XBC, P_DT, P_END = 0, 256, 384, 512, 1024, 1280, 2048, 2176

VMEM_LIMIT = 56 * 1024 * 1024


def _rms(x, g):
    ms = jnp.mean(x * x, axis=-1, keepdims=True)
    return x * lax.rsqrt(ms + EPS) * g


def _sigmoid(x):
    return 1.0 / (1.0 + jnp.exp(-x))


def _silu(x):
    return x * _sigmoid(x)


def _dot(a, b):
    return jnp.dot(a, b, preferred_element_type=F32)


def _dot_nt(a, b):
    return lax.dot_general(a, b, (((1,), (1,)), ((), ())), preferred_element_type=F32)


def _layer_spec(w, layer, single=False):
    idx = (layer,) + (0,) * (w.ndim - 1)
    mode = dict(pipeline_mode=pl.Buffered(1)) if single else {}
    return pl.BlockSpec((None,) + w.shape[1:], lambda *_: idx, **mode)


def _mod_spec(mods, layer, mod_row):
    if mod_row is None:
        return pl.BlockSpec((None, None, 1, mods.shape[-1]), lambda bi, i: (layer, bi, 0, 0))
    return pl.BlockSpec((None, None, 1, mods.shape[-1]), lambda bi, i: (layer, mod_row, 0, 0))


def _mod_kernel(c_ref, w_ref, b_ref, o_ref):
    s = _silu(c_ref[...]).astype(BF16)
    o_ref[...] = _dot(s, w_ref[...].astype(BF16)) + b_ref[...]


def _mod_call(cc, w_ada, b_ada):
    depth, d, n = w_ada.shape
    tn = 1536 if n % 1536 == 0 else 512
    return pl.pallas_call(
        _mod_kernel,
        out_shape=jax.ShapeDtypeStruct((depth, MOD_ROWS, n), F32),
        grid=(depth, n // tn),
        in_specs=[
            pl.BlockSpec((MOD_ROWS, d), lambda l, j: (0, 0)),
            pl.BlockSpec((None, d, tn), lambda l, j: (l, 0, j)),
            pl.BlockSpec((None, 1, tn), lambda l, j: (l, 0, j)),
        ],
        out_specs=pl.BlockSpec((None, MOD_ROWS, tn), lambda l, j: (l, 0, j)),
        compiler_params=pltpu.CompilerParams(dimension_semantics=("arbitrary", "arbitrary"),
                                             vmem_limit_bytes=VMEM_LIMIT),
        name="adaln_rows",
    )(cc, w_ada, b_ada.reshape(depth, 1, n))


def _pre_kernel(x_ref, xp_ref, xn_ref, mod_ref, gpre_ref, win_ref, gq_ref, wuq_ref, gkv_ref, wukv_ref,
                cos_ref, sin_ref, cmg_ref, cmw_ref, cmb_ref, convw_ref, convb_ref,
                q_ref, k_ref, vt_ref, ycm_ref, z_ref, xs_ref, cmo_ref, bmt_ref, dt_ref, h_scr):
    d = x_ref.shape[-1]
    tm = x_ref.shape[0]
    i = pl.program_id(1)
    last_tile = pl.num_programs(1) - 1
    shift = mod_ref[:, 0:d]
    scale = mod_ref[:, d:2 * d]
    lane_group = lax.broadcasted_iota(jnp.int32, (CHUNK, CM_WIDTH), 1) // CM_GROUP_DIM
    sub = min(tm, SUB_ROWS)
    row_id = lax.broadcasted_iota(jnp.int32, (sub, SSD_CONV_DIM), 0)
    cw = convw_ref[...]

    def normed(xv):
        return (_rms(xv, gpre_ref[...]) * (1.0 + scale) + shift).astype(BF16)

    h_scr[0:HALO, :] = normed(xp_ref[...])
    h_scr[HALO + tm:2 * HALO + tm, :] = normed(xn_ref[...])
    h_scr[HALO:HALO + sub, :] = normed(x_ref[0:sub, :])
    for r0 in range(0, tm, sub):
        rs_ = slice(r0, r0 + sub)
        if r0 + sub < tm:
            h_scr[HALO + r0 + sub:HALO + r0 + 2 * sub, :] = normed(x_ref[r0 + sub:r0 + 2 * sub, :])
        h = h_scr[HALO + r0:HALO + r0 + sub, :]

        def proj(lo, hi):
            return _dot(h, win_ref[:, lo:hi])

        cos = cos_ref[rs_, :]
        sin = sin_ref[rs_, :]

        def rotate(rs):
            return rs * cos + pltpu.roll(rs, ROPE, 1) * sin

        q_lo = proj(P_Q, P_KV)
        kvkr = proj(P_KV, P_CM)
        xcm = proj(P_CM, P_Z)
        xbc = _dot(h_scr[r0:r0 + sub + 2 * HALO, :], win_ref[:, P_XBC:P_DT])
        z_ref[rs_, :] = proj(P_Z, P_XBC)
        dt_ref[rs_, :] = proj(P_DT, P_END)

        qn = _rms(q_lo, gq_ref[...]).astype(BF16)
        qf = _dot(qn, wuq_ref[...])
        kvn = _rms(kvkr[:, :KV_LORA], gkv_ref[...]).astype(BF16)
        kvf = _dot(kvn, wukv_ref[...])
        for hh in range(HEADS):
            lo = hh * QK_PAD
            q_ref[rs_, lo:lo + NOPE] = qf[:, lo:lo + NOPE].astype(BF16)
            q_ref[rs_, lo + NOPE:lo + QK_PAD] = rotate(qf[:, lo + NOPE:lo + QK_PAD]).astype(BF16)
        krot = rotate(kvkr[:, KV_LORA:]).astype(BF16)
        for hh in range(HEADS):
            lo = hh * QK_PAD
            k_ref[rs_, lo:lo + NOPE] = kvf[:, hh * (NOPE + VDIM):hh * (NOPE + VDIM) + NOPE].astype(BF16)
            k_ref[rs_, lo + NOPE:lo + QK_PAD] = krot
            vt_ref[hh * VDIM:(hh + 1) * VDIM, rs_] = (
                kvf[:, hh * (NOPE + VDIM) + NOPE:(hh + 1) * (NOPE + VDIM)].T.astype(BF16))

        n_ext = sub + 2 * HALO
        x_prev = pltpu.roll(xbc, 1, 0)[HALO:HALO + sub, :]
        x_next = pltpu.roll(xbc, n_ext - 1, 0)[HALO:HALO + sub, :]
        if r0 == 0:
            x_prev = jnp.where((row_id == 0) & (i == 0), 0.0, x_prev)
        if r0 + sub == tm:
            x_next = jnp.where((row_id == sub - 1) & (i == last_tile), 0.0, x_next)
        xa = _silu(cw[0:1, :] * x_prev + cw[1:2, :] * xbc[HALO:HALO + sub, :] + cw[2:3, :] * x_next + convb_ref[...])
        xs_ref[rs_, :] = xa[:, :SSD_WIDTH]
        cmo_ref[rs_, :] = xa[:, SSD_WIDTH + SSD_GROUPS * SSD_STATE:].astype(BF16)
        for g in range(SSD_GROUPS):
            for c in range(sub // CHUNK):
                blk = xa[c * CHUNK:(c + 1) * CHUNK, SSD_WIDTH + g * SSD_STATE:SSD_WIDTH + (g + 1) * SSD_STATE]
                bmt_ref[g * SSD_STATE:(g + 1) * SSD_STATE, r0 + c * CHUNK:r0 + (c + 1) * CHUNK] = blk.T.astype(BF16)

        gl = xcm * (0.5 * (1.0 + jnp.tanh(math.sqrt(2.0 / math.pi) * (xcm + 0.044715 * (xcm * xcm * xcm)))))
        u = gl[:, :CM_WIDTH]
        vv = gl[:, CM_WIDTH:]
        mu = jnp.mean(vv, axis=-1, keepdims=True)
        vc = vv - mu
        var = jnp.mean(vc * vc, axis=-1, keepdims=True)
        vn = vc * lax.rsqrt(var + EPS) * cmg_ref[...]
        for c in range(sub // CHUNK):
            rows = slice(c * CHUNK, (c + 1) * CHUNK)
            vchunk = vn[rows, :]
            vblk = jnp.concatenate(
                [jnp.where(lane_group == g, vchunk, 0.0).astype(BF16) for g in range(CM_GROUPS)], axis=0)
            mixed = _dot(cmw_ref[...], vblk) + cmb_ref[...]
            ycm_ref[r0 + c * CHUNK:r0 + (c + 1) * CHUNK, :] = (u[rows, :] * mixed).astype(BF16)


def _pre_call(x, mods, layer, mod_row, pk, cos, sin, tm):
    b, l, d = x.shape
    nt = l // tm
    per = tm // HALO
    last_halo = l // HALO - 1
    row_spec = lambda w: pl.BlockSpec((None, tm, w), lambda bi, i: (bi, i, 0))
    col_spec = lambda h: pl.BlockSpec((None, h, tm), lambda bi, i: (bi, 0, i))
    halo_prev = pl.BlockSpec((None, HALO, d), lambda bi, i: (bi, jnp.maximum(i * per - 1, 0), 0))
    halo_next = pl.BlockSpec((None, HALO, d), lambda bi, i: (bi, jnp.minimum((i + 1) * per, last_halo), 0))
    tab_spec = pl.BlockSpec((tm, LANE), lambda bi, i: (i, 0))
    weights = [pk[n] for n in ("g_pre_mix", "w_in", "g_q", "w_uq", "g_kv", "w_ukv")]
    tail_weights = [pk[n] for n in ("cm_norm_g", "cm_w", "cm_b", "ssd_conv_w", "ssd_conv_b")]
    outs = [((l, HEADS * QK_PAD), BF16, row_spec(HEADS * QK_PAD)), ((l, HEADS * QK_PAD), BF16, row_spec(HEADS * QK_PAD)),
            ((HEADS * VDIM, l), BF16, col_spec(HEADS * VDIM)), ((l, CM_WIDTH), BF16, row_spec(CM_WIDTH)),
            ((l, SSD_WIDTH), F32, row_spec(SSD_WIDTH)), ((l, SSD_WIDTH), F32, row_spec(SSD_WIDTH)),
            ((l, SSD_GROUPS * SSD_STATE), BF16, row_spec(SSD_GROUPS * SSD_STATE)),
            ((SSD_GROUPS * SSD_STATE, l), BF16, col_spec(SSD_GROUPS * SSD_STATE)), ((l, LANE), F32, row_spec(LANE))]
    return pl.pallas_call(
        _pre_kernel,
        out_shape=[jax.ShapeDtypeStruct((b,) + s, dt) for s, dt, _ in outs],
        grid=(b, nt),
        in_specs=[row_spec(d), halo_prev, halo_next, _mod_spec(mods, layer, mod_row)]
        + [_layer_spec(w, layer) for w in weights] + [tab_spec, tab_spec]
        + [_layer_spec(w, layer) for w in tail_weights],
        out_specs=[sp for _, _, sp in outs],
        scratch_shapes=[pltpu.VMEM((tm + 2 * HALO, d), BF16)],
        compiler_params=pltpu.CompilerParams(dimension_semantics=("arbitrary", "arbitrary"),
                                             vmem_limit_bytes=VMEM_LIMIT),
        name="pre_mix",
    )(x, x, x, mods, *weights, cos, sin, *tail_weights)


def _attn_kernel(q_ref, *refs, tk):
    o_ref, s_scr = refs[-2:]
    kv_refs = refs[:-2]
    tq = q_ref.shape[0]
    q = q_ref[...]
    tiles = []
    for seg in range(len(kv_refs) // 2):
        k_ref, vt_ref = kv_refs[2 * seg], kv_refs[2 * seg + 1]
        n = k_ref.shape[0]
        step = min(tk, n)
        tiles += [(k_ref, vt_ref, start, step) for start in range(0, n, step)]

    def scores_into(tile, slot):
        k_ref, _, start, step = tile
        s = _dot_nt(k_ref[start:start + step, :], q)
        s_scr[slot, 0:step, :] = s
        return jnp.max(s, axis=0, keepdims=True)

    m = jnp.full((1, tq), -1e30, F32)
    acc = jnp.zeros((VDIM + ONES_ROWS, tq), F32)
    mx = scores_into(tiles[0], 0)
    for t, (_, vt_ref, start, step) in enumerate(tiles):
        slot = t % 2
        mx_next = scores_into(tiles[t + 1], 1 - slot) if t + 1 < len(tiles) else None
        m_new = jnp.maximum(m, mx)
        alpha = jnp.exp2(m - m_new)
        p = jnp.exp2(s_scr[slot, 0:step, :] - m_new).astype(BF16)
        vt_ext = jnp.concatenate([vt_ref[:, start:start + step], jnp.ones((ONES_ROWS, step), BF16)], axis=0)
        acc = alpha * acc + _dot(vt_ext, p)
        m = m_new
        mx = mx_next
    o_ref[...] = (acc[0:VDIM, :] * (1.0 / acc[VDIM:VDIM + 1, :])).T.astype(BF16)


def _attn_call(q, kvs, tq, tk):
    b, lq, _ = q.shape
    in_specs = [pl.BlockSpec((None, tq, QK_PAD), lambda bi, hi, i: (bi, i, hi))]
    args = [q]
    for k, vt in kvs:
        lk = k.shape[1]
        in_specs.append(pl.BlockSpec((None, lk, QK_PAD), lambda bi, hi, i: (bi, 0, hi)))
        in_specs.append(pl.BlockSpec((None, VDIM, lk), lambda bi, hi, i: (bi, hi, 0)))
        args += [k, vt]
    return pl.pallas_call(
        functools.partial(_attn_kernel, tk=tk),
        out_shape=jax.ShapeDtypeStruct((b, lq, HEADS * VDIM), BF16),
        grid=(b, HEADS, lq // tq),
        in_specs=in_specs,
        out_specs=pl.BlockSpec((None, tq, VDIM), lambda bi, hi, i: (bi, i, hi)),
        scratch_shapes=[pltpu.VMEM((2, tk, tq), F32)],
        compiler_params=pltpu.CompilerParams(dimension_semantics=("arbitrary", "arbitrary", "arbitrary"),
                                             vmem_limit_bytes=VMEM_LIMIT),
        name="mla_attention",
    )(*args)


def _ssd_kernel(xs_f, cm_f, bmt_f, dt_f, xs_b, cm_b_, bmt_b, dt_b,
                dtbias_ref, alog_ref, drep_ref, init_ref,
                yf_ref, yb_ref, fin_ref, st_ref, *, nc):
    j = pl.program_id(1)

    @pl.when(j == 0)
    def _():
        st_ref[...] = init_ref[...]

    nb = xs_f.shape[0]
    a_row = -jnp.exp(alog_ref[...])
    ri = lax.broadcasted_iota(jnp.int32, (CHUNK, CHUNK), 0)
    ci = lax.broadcasted_iota(jnp.int32, (CHUNK, CHUNK), 1)
    low_half = ci < (LANE // 2)
    causal = ((ri >= ci), (ri <= ci))
    tri = tuple(jnp.where(cz, 1.0, 0.0).astype(BF16) for cz in causal)
    streams = ((xs_f, cm_f, bmt_f, dt_f, yf_ref), (xs_b, cm_b_, bmt_b, dt_b, yb_ref))
    chains = [(n, d) for n in range(nb) for d in range(2)]
    groups = [(n, d, g) for (n, d) in chains for g in range(SSD_GROUPS)]

    dtv, a_parts = {}, {}
    for n, d in chains:
        dt_in = streams[d][3][n] + dtbias_ref[...]
        dtv[n, d] = jnp.maximum(dt_in, 0.0) + jnp.log1p(jnp.exp(-jnp.abs(dt_in)))
        a = dtv[n, d] * a_row
        a_hi = a.astype(BF16)
        r1 = a - a_hi.astype(F32)
        a_mid = r1.astype(BF16)
        a_parts[n, d] = (a_hi, a_mid, (r1 - a_mid.astype(F32)).astype(BF16))

    cum = {}
    for n, d in chains:
        a_hi, a_mid, a_lo = a_parts[n, d]
        cum[n, d] = _dot(tri[d], a_hi) + _dot(tri[d], a_mid) + _dot(tri[d], a_lo)

    cmb, gmat, y_off_raw, st_old = {}, {}, {}, {}
    for n, d, g in groups:
        gl = slice(g * LANE, (g + 1) * LANE)
        cmb[n, d, g] = streams[d][1][n, :, gl]
        gmat[n, d, g] = _dot(cmb[n, d, g], streams[d][2][n, g * SSD_STATE:(g + 1) * SSD_STATE, :])
        st_old[n, d, g] = st_ref[n, d, g]
        y_off_raw[n, d, g] = _dot(cmb[n, d, g], st_old[n, d, g].astype(BF16))

    cum_t, tot = {}, {}
    for n, d in chains:
        cum_t[n, d] = cum[n, d].T
        tot[n, d] = cum[n, d][CHUNK - 1:CHUNK, :] if d == 0 else cum[n, d][0:1, :]
    colx, totx, xdt_b, xdec_b, scores = {}, {}, {}, {}, {}
    for n, d, g in groups:
        i0 = d * SSD_HEADS + 2 * g
        i1 = i0 + 1
        gl = slice(g * LANE, (g + 1) * LANE)
        c_nd = cum[n, d]
        colx[n, d, g] = jnp.where(low_half, c_nd[:, i0:i0 + 1], c_nd[:, i1:i1 + 1])
        dtx = jnp.where(low_half, dtv[n, d][:, i0:i0 + 1], dtv[n, d][:, i1:i1 + 1])
        totx[n, d, g] = jnp.where(low_half[0:1, :], tot[n, d][:, i0:i0 + 1], tot[n, d][:, i1:i1 + 1])
        xdt = streams[d][0][n, :, gl] * dtx
        xdt_b[n, d, g] = xdt.astype(BF16)
        xdec_b[n, d, g] = (xdt * jnp.exp(totx[n, d, g] - colx[n, d, g])).astype(BF16)
        for hh, ii in enumerate((i0, i1)):
            diff = c_nd[:, ii:ii + 1] - cum_t[n, d][ii:ii + 1, :]
            lmat = jnp.exp(jnp.where(causal[d], diff, -1e30))
            scores[n, d, g, hh] = (gmat[n, d, g] * lmat).astype(BF16)

    ys, st_add = {}, {}
    for n, d, g in groups:
        ys[n, d, g] = [_dot(scores[n, d, g, hh], xdt_b[n, d, g]) for hh in range(2)]
        st_add[n, d, g] = _dot(streams[d][2][n, g * SSD_STATE:(g + 1) * SSD_STATE, :], xdec_b[n, d, g])

    for n, d, g in groups:
        gl = slice(g * LANE, (g + 1) * LANE)
        y = jnp.where(low_half, ys[n, d, g][0], ys[n, d, g][1]) + y_off_raw[n, d, g] * jnp.exp(colx[n, d, g])
        if d == 0:
            y = y + (drep_ref[0:1, gl] + drep_ref[1:2, gl]) * streams[d][0][n, :, gl]
        streams[d][4][n, :, gl] = y
        st_ref[n, d, g] = jnp.exp(totx[n, d, g]) * st_old[n, d, g] + st_add[n, d, g]

    @pl.when(j == nc - 1)
    def _():
        fin_ref[...] = st_ref[...]


def _ssd_call(xs, cmo, bmt, dt, pk, layer, init, nb_want=8):
    b, l, _ = xs.shape
    nc = l // CHUNK
    fwd = lambda bi, j: (bi, j, 0)
    bwd = lambda bi, j: (bi, nc - 1 - j, 0)
    fwd_t = lambda bi, j: (bi, 0, j)
    bwd_t = lambda bi, j: (bi, 0, nc - 1 - j)
    nb = nb_want if b % nb_want == 0 else 1
    rows = lambda w, f: pl.BlockSpec((nb, CHUNK, w), f)
    cols = lambda h, f: pl.BlockSpec((nb, h, CHUNK), f)
    small = [pk[n] for n in ("ssd_dt_bias", "ssd_a_log", "ssd_d")]
    st_shape = (2, SSD_GROUPS, SSD_STATE, LANE)
    st_spec = pl.BlockSpec((nb,) + st_shape, lambda bi, j: (bi, 0, 0, 0, 0))
    y_shape = jax.ShapeDtypeStruct((b, l, SSD_WIDTH), F32)
    per_dir = lambda f, ft: [rows(SSD_WIDTH, f), rows(SSD_GROUPS * SSD_STATE, f),
                             cols(SSD_GROUPS * SSD_STATE, ft), rows(LANE, f)]
    return pl.pallas_call(
        functools.partial(_ssd_kernel, nc=nc),
        out_shape=[y_shape, y_shape, jax.ShapeDtypeStruct((b,) + st_shape, F32)],
        grid=(b // nb, nc),
        in_specs=per_dir(fwd, fwd_t) + per_dir(bwd, bwd_t) + [_layer_spec(w, layer) for w in small] + [st_spec],
        out_specs=[rows(SSD_WIDTH, fwd), rows(SSD_WIDTH, bwd), st_spec],
        scratch_shapes=[pltpu.VMEM((nb,) + st_shape, F32)],
        compiler_params=pltpu.CompilerParams(dimension_semantics=("arbitrary", "arbitrary"),
                                             vmem_limit_bytes=VMEM_LIMIT),
        name="ssd_scan",
    )(xs, cmo, bmt, dt, xs, cmo, bmt, dt, *small, init)


def _post_kernel(x_ref, ya_ref, ycm_ref, yf_ref, yb_ref, z_ref, mod_ref,
                 gssd_ref, gpm_ref, gpf_ref, gpo_ref, wout_ref, w1_ref, w2_ref, o_ref, *, tf):
    d = x_ref.shape[-1]
    f = w1_ref.shape[-1]
    gate1 = mod_ref[:, 2 * d:3 * d]
    shift2 = mod_ref[:, 3 * d:4 * d]
    scale2 = mod_ref[:, 4 * d:5 * d]
    gate2 = mod_ref[:, 5 * d:6 * d]
    wa = HEADS * VDIM
    tm = x_ref.shape[0]
    sub = min(tm, SUB_ROWS)

    def mixed(r0):
        rs_ = slice(r0, r0 + sub)
        y_ssd = _rms((yf_ref[rs_, :] + yb_ref[rs_, :]) * _silu(z_ref[rs_, :]), gssd_ref[...]).astype(BF16)
        y = (_dot(ya_ref[rs_, :], wout_ref[0:wa, :]) + _dot(ycm_ref[rs_, :], wout_ref[wa:wa + CM_WIDTH, :])
             + _dot(y_ssd, wout_ref[wa + CM_WIDTH:, :]))
        x1 = x_ref[rs_, :] + gate1 * _rms(y, gpm_ref[...])
        return x1, (_rms(x1, gpf_ref[...]) * (1.0 + scale2) + shift2).astype(BF16)

    nxt = mixed(0)
    for r0 in range(0, tm, sub):
        rs_ = slice(r0, r0 + sub)
        x1, h2 = nxt
        if r0 + sub < tm:
            nxt = mixed(r0 + sub)
        acc = jnp.zeros(x1.shape, F32)
        for c in range(f // tf):
            hc = jnp.maximum(_dot(h2, w1_ref[:, c * tf:(c + 1) * tf]), 0.0)
            acc = acc + _dot((hc * hc).astype(BF16), w2_ref[c * tf:(c + 1) * tf, :])
        o_ref[rs_, :] = x1 + gate2 * _rms(acc, gpo_ref[...])


def _post_call(x, ya, ycm, yf, yb, z, mods, layer, mod_row, pk, tm):
    b, l, d = x.shape
    row_spec = lambda w: pl.BlockSpec((None, tm, w), lambda bi, i: (bi, i, 0))
    weights = [pk[n] for n in ("ssd_norm_g", "g_post_mix", "g_pre_ff", "g_post_ff", "w_out", "w_ff1", "w_ff2")]
    return pl.pallas_call(
        functools.partial(_post_kernel, tf=1024),
        out_shape=jax.ShapeDtypeStruct((b, l, d), F32),
        grid=(b, l // tm),
        in_specs=[row_spec(d), row_spec(HEADS * VDIM), row_spec(CM_WIDTH), row_spec(SSD_WIDTH), row_spec(SSD_WIDTH),
                  row_spec(SSD_WIDTH), _mod_spec(mods, layer, mod_row)]
        + [_layer_spec(w, layer, single=True) for w in weights],
        out_specs=row_spec(d),
        compiler_params=pltpu.CompilerParams(dimension_semantics=("arbitrary", "arbitrary"),
                                             vmem_limit_bytes=VMEM_LIMIT),
        name="post_mix_ffn",
    )(x, ya, ycm, yf, yb, z, mods, *weights)


def _rope_tables(n_tokens):
    rows_n = n_tokens // GRID_W
    axis_dim = ROPE // 2
    inv_freq = ROPE_BASE ** (-jnp.arange(0, axis_dim, 2, dtype=F32) / axis_dim)
    ang_r = jnp.arange(rows_n, dtype=F32)[:, None] * inv_freq
    ang_c = jnp.arange(GRID_W, dtype=F32)[:, None] * inv_freq
    per_row = lambda t: jnp.repeat(t, GRID_W, axis=0)
    per_col = lambda t: jnp.tile(t, (rows_n, 1))
    cos_r, sin_r = per_row(jnp.cos(ang_r)), per_row(jnp.sin(ang_r))
    cos_c, sin_c = per_col(jnp.cos(ang_c)), per_col(jnp.sin(ang_c))
    zeros = jnp.zeros((n_tokens, LANE - ROPE), F32)
    cos = jnp.concatenate([cos_r, cos_r, cos_c, cos_c, zeros], axis=-1)
    sin = jnp.concatenate([-sin_r, sin_r, -sin_c, sin_c, zeros], axis=-1)
    return cos, sin


def _rotate_half_partner():
    q = ROPE // 4
    return np.concatenate([np.arange(q, 2 * q), np.arange(0, q), np.arange(3 * q, 4 * q), np.arange(2 * q, 3 * q)])


def _pack_weights(p):
    depth = p["w_in"].shape[0]
    partner = _rotate_half_partner()
    o_kr = Q_LORA + KV_LORA
    o_cm = o_kr + ROPE
    o_dt = o_cm + 2 * CM_WIDTH + SSD_WIDTH + SSD_CONV_DIM
    w_in = p["w_in"].astype(BF16)
    kr = w_in[..., o_kr:o_cm]
    pad = jnp.zeros(w_in.shape[:-1] + (LANE - 2 * SSD_HEADS,), w_in.dtype)
    w_in_p = jnp.concatenate([w_in[..., :o_kr], kr, kr[..., partner], w_in[..., o_cm:o_dt], w_in[..., o_dt:], pad],
                             axis=-1)
    assert w_in_p.shape[-1] == P_END
    w_uq = p["w_uq"].astype(BF16).reshape(depth, Q_LORA, HEADS, NOPE + ROPE)
    rope_cols = w_uq[..., NOPE:]
    w_uq_p = jnp.concatenate([w_uq[..., :NOPE], rope_cols, rope_cols[..., partner]], axis=-1)
    row = lambda a: a.reshape(depth, 1, -1)
    pad_lane = lambda a: jnp.pad(a.reshape(depth, 1, -1), ((0, 0), (0, 0), (0, LANE - 2 * SSD_HEADS)))
    return {
        "g_pre_mix": row(p["g_pre_mix"]), "g_post_mix": row(p["g_post_mix"]),
        "g_pre_ff": row(p["g_pre_ff"]), "g_post_ff": row(p["g_post_ff"]),
        "w_in": w_in_p,
        "g_q": row(p["g_q"]) * QSCALE, "w_uq": w_uq_p.reshape(depth, Q_LORA, HEADS * QK_PAD),
        "g_kv": row(p["g_kv"]), "w_ukv": p["w_ukv"].astype(BF16),
        "cm_norm_g": row(p["cm_norm_g"]),
        "cm_w": jnp.transpose(p["cm_w_s"], (0, 2, 1, 3)).reshape(depth, CHUNK, CM_GROUPS * CHUNK).astype(BF16),
        "cm_b": jnp.repeat(jnp.transpose(p["cm_b_s"], (0, 2, 1)), CM_GROUP_DIM, axis=-1),
        "ssd_conv_w": p["ssd_conv_w"], "ssd_conv_b": row(p["ssd_conv_b"]),
        "ssd_dt_bias": pad_lane(p["ssd_dt_bias"]), "ssd_a_log": pad_lane(p["ssd_a_log"]),
        "ssd_d": jnp.repeat(p["ssd_d"], SSD_WIDTH // SSD_HEADS, axis=-1),
        "ssd_norm_g": row(p["ssd_norm_g"]),
        "w_out": p["w_out"].astype(BF16), "w_ff1": p["w_ff1"].astype(BF16), "w_ff2": p["w_ff2"].astype(BF16),
    }


def _row_tile(l, want):
    return want if l % want == 0 else l


def _tile_plan(n_lat, n_ctx):
    return _row_tile(n_lat, 4 * SUB_ROWS), _row_tile(n_ctx, SUB_ROWS), _row_tile(n_lat, 4 * SUB_ROWS), 512


def kernel(x, c, ctx, c_ctx, w_ada, b_ada, g_pre_mix, g_post_mix, g_pre_ff, g_post_ff, w_in, g_q, w_uq, g_kv, w_ukv, cm_norm_g, cm_w_s, cm_b_s, ssd_conv_w, ssd_conv_b, ssd_dt_bias, ssd_a_log, ssd_d, ssd_norm_g, w_out, w_ff1, w_ff2):
    b, n_lat, d = x.shape
    n_ctx = ctx.shape[1]
    depth = w_ada.shape[0]
    assert b + 1 <= MOD_ROWS and n_lat % CHUNK == 0 and n_ctx % CHUNK == 0 and n_lat % GRID_W == 0

    pk = _pack_weights(dict(
        g_pre_mix=g_pre_mix, g_post_mix=g_post_mix, g_pre_ff=g_pre_ff, g_post_ff=g_post_ff, w_in=w_in, g_q=g_q,
        w_uq=w_uq, g_kv=g_kv, w_ukv=w_ukv, cm_norm_g=cm_norm_g, cm_w_s=cm_w_s, cm_b_s=cm_b_s,
        ssd_conv_w=ssd_conv_w, ssd_conv_b=ssd_conv_b, ssd_dt_bias=ssd_dt_bias, ssd_a_log=ssd_a_log, ssd_d=ssd_d,
        ssd_norm_g=ssd_norm_g, w_out=w_out, w_ff1=w_ff1, w_ff2=w_ff2))

    cc = jnp.concatenate([c, c_ctx[None, :], jnp.zeros((MOD_ROWS - b - 1, d), c.dtype)], axis=0)
    mods = _mod_call(cc, w_ada, b_ada).reshape(depth, MOD_ROWS, 1, 6 * d)

    cos_l, sin_l = _rope_tables(n_lat)
    cos_c = jnp.concatenate([jnp.ones((n_ctx, ROPE), F32), jnp.zeros((n_ctx, LANE - ROPE), F32)], axis=-1)
    sin_c = jnp.zeros((n_ctx, LANE), F32)

    tm_l, tm_c, tq_l, tk = _tile_plan(n_lat, n_ctx)
    zero_states = jnp.zeros((b, 2, SSD_GROUPS, SSD_STATE, LANE), F32)

    xc = ctx
    for layer in range(depth):
        last = layer == depth - 1

        cq, ck, cv, cycm, cz, cxs, ccm, cbmt, cdt = _pre_call(xc, mods, layer, b, pk, cos_c, sin_c, tm_c)
        cyf, cyb, ctx_states = _ssd_call(cxs, ccm, cbmt, cdt, pk, layer, zero_states)
        if not last:
            cya = _attn_call(cq, [(ck, cv)], tm_c, tk)
            xc = _post_call(xc, cya, cycm, cyf, cyb, cz, mods, layer, b, pk, tm_c)

        q, k, v, ycm, z, xs, cmo, bmt, dt = _pre_call(x, mods, layer, None, pk, cos_l, sin_l, tm_l)
        yf, yb, _ = _ssd_call(xs, cmo, bmt, dt, pk, layer, ctx_states)
        ya = _attn_call(q, [(k, v), (ck, cv)], tq_l, tk)
        x = _post_call(x, ya, ycm, yf, yb, z, mods, layer, None, pk, tm_l)
    return x
```

```python
import functools
import math

import numpy as np
import jax
import jax.numpy as jnp
from jax import lax
from jax.experimental import pallas as pl
from jax.experimental.pallas import tpu as pltpu

F32 = jnp.float32
BF16 = jnp.bfloat16

EPS = 1e-6
GRID_W = 64
ROPE_BASE = 10000.0

HEADS = 4
Q_LORA = 256
KV_LORA = 128
NOPE = 128
ROPE = 64
VDIM = 128
QK_PAD = 256
ATTN_SCALE = (NOPE + ROPE) ** -0.5
QSCALE = ATTN_SCALE * math.log2(math.e)

CM_GROUPS = 4
CM_WIDTH = 256
CM_GROUP_DIM = CM_WIDTH // CM_GROUPS
CHUNK = 128

SSD_WIDTH = 256
SSD_HEADS = 4
SSD_GROUPS = 2
SSD_STATE = 128
SSD_CONV_DIM = SSD_WIDTH + 2 * SSD_GROUPS * SSD_STATE

LANE = 128
HALO = 16
HEADS_PER_STEP = 2
ONES_ROWS = 16
MOD_ROWS = 16
SUB_ROWS = 256

P_Q, P_KV, P_KR, P_CM, P_Z, P_XBC, P_DT, P_END = 0, 256, 384, 512, 1024, 1280, 2048, 2176

VMEM_LIMIT = 56 * 1024 * 1024


def _rms(x, g):
    ms = jnp.mean(x * x, axis=-1, keepdims=True)
    return x * lax.rsqrt(ms + EPS) * g


def _sigmoid(x):
    return 1.0 / (1.0 + jnp.exp(-x))


def _silu(x):
    return x * _sigmoid(x)


def _dot(a, b):
    return jnp.dot(a, b, preferred_element_type=F32)


def _dot_nt(a, b):
    return lax.dot_general(a, b, (((1,), (1,)), ((), ())), preferred_element_type=F32)


def _layer_spec(w, layer, single=False):
    idx = (layer,) + (0,) * (w.ndim - 1)
    mode = dict(pipeline_mode=pl.Buffered(1)) if single else {}
    return pl.BlockSpec((None,) + w.shape[1:], lambda *_: idx, **mode)


def _mod_spec(mods, layer, mod_row):
    if mod_row is None:
        return pl.BlockSpec((None, None, 1, mods.shape[-1]), lambda bi, i: (layer, bi, 0, 0))
    return pl.BlockSpec((None, None, 1, mods.shape[-1]), lambda bi, i: (layer, mod_row, 0, 0))


def _mod_kernel(c_ref, w_ref, b_ref, o_ref):
    s = _silu(c_ref[...]).astype(BF16)
    o_ref[...] = _dot(s, w_ref[...].astype(BF16)) + b_ref[...]


def _mod_call(cc, w_ada, b_ada):
    depth, d, n = w_ada.shape
    tn = 1536 if n % 1536 == 0 else 512
    return pl.pallas_call(
        _mod_kernel,
        out_shape=jax.ShapeDtypeStruct((depth, MOD_ROWS, n), F32),
        grid=(depth, n // tn),
        in_specs=[
            pl.BlockSpec((MOD_ROWS, d), lambda l, j: (0, 0)),
            pl.BlockSpec((None, d, tn), lambda l, j: (l, 0, j)),
            pl.BlockSpec((None, 1, tn), lambda l, j: (l, 0, j)),
        ],
        out_specs=pl.BlockSpec((None, MOD_ROWS, tn), lambda l, j: (l, 0, j)),
        compiler_params=pltpu.CompilerParams(dimension_semantics=("arbitrary", "arbitrary"),
                                             vmem_limit_bytes=VMEM_LIMIT),
        name="adaln_rows",
    )(cc, w_ada, b_ada.reshape(depth, 1, n))


def _pre_kernel(x_ref, xp_ref, xn_ref, mod_ref, gpre_ref, win_ref, gq_ref, wuq_ref, gkv_ref, wukv_ref,
                cos_ref, sin_ref, cmg_ref, cmw_ref, cmb_ref, convw_ref, convb_ref,
                q_ref, k_ref, vt_ref, ycm_ref, z_ref, xs_ref, cmo_ref, bmt_ref, dt_ref, h_scr):
    d = x_ref.shape[-1]
    tm = x_ref.shape[0]
    i = pl.program_id(1)
    last_tile = pl.num_programs(1) - 1
    shift = mod_ref[:, 0:d]
    scale = mod_ref[:, d:2 * d]
    lane_group = lax.broadcasted_iota(jnp.int32, (CHUNK, CM_WIDTH), 1) // CM_GROUP_DIM
    sub = min(tm, SUB_ROWS)
    row_id = lax.broadcasted_iota(jnp.int32, (sub, SSD_CONV_DIM), 0)
    cw = convw_ref[...]

    def normed(xv):
        return (_rms(xv, gpre_ref[...]) * (1.0 + scale) + shift).astype(BF16)

    h_scr[0:HALO, :] = normed(xp_ref[...])
    h_scr[HALO + tm:2 * HALO + tm, :] = normed(xn_ref[...])
    h_scr[HALO:HALO + sub, :] = normed(x_ref[0:sub, :])
    for r0 in range(0, tm, sub):
        rs_ = slice(r0, r0 + sub)
        if r0 + sub < tm:
            h_scr[HALO + r0 + sub:HALO + r0 + 2 * sub, :] = normed(x_ref[r0 + sub:r0 + 2 * sub, :])
        h = h_scr[HALO + r0:HALO + r0 + sub, :]

        def proj(lo, hi):
            return _dot(h, win_ref[:, lo:hi])

        cos = cos_ref[rs_, :]
        sin = sin_ref[rs_, :]

        def rotate(rs):
            return rs * cos + pltpu.roll(rs, ROPE, 1) * sin

        q_lo = proj(P_Q, P_KV)
        kvkr = proj(P_KV, P_CM)
        xcm = proj(P_CM, P_Z)
        xbc = _dot(h_scr[r0:r0 + sub + 2 * HALO, :], win_ref[:, P_XBC:P_DT])
        z_ref[rs_, :] = proj(P_Z, P_XBC)
        dt_ref[rs_, :] = proj(P_DT, P_END)

        qn = _rms(q_lo, gq_ref[...]).astype(BF16)
        qf = _dot(qn, wuq_ref[...])
        kvn = _rms(kvkr[:, :KV_LORA], gkv_ref[...]).astype(BF16)
        kvf = _dot(kvn, wukv_ref[...])
        for hh in range(HEADS):
            lo = hh * QK_PAD
            q_ref[rs_, lo:lo + NOPE] = qf[:, lo:lo + NOPE].astype(BF16)
            q_ref[rs_, lo + NOPE:lo + QK_PAD] = rotate(qf[:, lo + NOPE:lo + QK_PAD]).astype(BF16)
        krot = rotate(kvkr[:, KV_LORA:]).astype(BF16)
        for hh in range(HEADS):
            lo = hh * QK_PAD
            k_ref[rs_, lo:lo + NOPE] = kvf[:, hh * (NOPE + VDIM):hh * (NOPE + VDIM) + NOPE].astype(BF16)
            k_ref[rs_, lo + NOPE:lo + QK_PAD] = krot
            vt_ref[hh * VDIM:(hh + 1) * VDIM, rs_] = (
                kvf[:, hh * (NOPE + VDIM) + NOPE:(hh + 1) * (NOPE + VDIM)].T.astype(BF16))

        n_ext = sub + 2 * HALO
        x_prev = pltpu.roll(xbc, 1, 0)[HALO:HALO + sub, :]
        x_next = pltpu.roll(xbc, n_ext - 1, 0)[HALO:HALO + sub, :]
        if r0 == 0:
            x_prev = jnp.where((row_id == 0) & (i == 0), 0.0, x_prev)
        if r0 + sub == tm:
            x_next = jnp.where((row_id == sub - 1) & (i == last_tile), 0.0, x_next)
        xa = _silu(cw[0:1, :] * x_prev + cw[1:2, :] * xbc[HALO:HALO + sub, :] + cw[2:3, :] * x_next + convb_ref[...])
        xs_ref[rs_, :] = xa[:, :SSD_WIDTH]
        cmo_ref[rs_, :] = xa[:, SSD_WIDTH + SSD_GROUPS * SSD_STATE:].astype(BF16)
        for g in range(SSD_GROUPS):
            for c in range(sub // CHUNK):
                blk = xa[c * CHUNK:(c + 1) * CHUNK, SSD_WIDTH + g * SSD_STATE:SSD_WIDTH + (g + 1) * SSD_STATE]
                bmt_ref[g * SSD_STATE:(g + 1) * SSD_STATE, r0 + c * CHUNK:r0 + (c + 1) * CHUNK] = blk.T.astype(BF16)

        gl = xcm * (0.5 * (1.0 + jnp.tanh(math.sqrt(2.0 / math.pi) * (xcm + 0.044715 * (xcm * xcm * xcm)))))
        u = gl[:, :CM_WIDTH]
        vv = gl[:, CM_WIDTH:]
        mu = jnp.mean(vv, axis=-1, keepdims=True)
        vc = vv - mu
        var = jnp.mean(vc * vc, axis=-1, keepdims=True)
        vn = vc * lax.rsqrt(var + EPS) * cmg_ref[...]
        for c in range(sub // CHUNK):
            rows = slice(c * CHUNK, (c + 1) * CHUNK)
            vchunk = vn[rows, :]
            vblk = jnp.concatenate(
                [jnp.where(lane_group == g, vchunk, 0.0).astype(BF16) for g in range(CM_GROUPS)], axis=0)
            mixed = _dot(cmw_ref[...], vblk) + cmb_ref[...]
            ycm_ref[r0 + c * CHUNK:r0 + (c + 1) * CHUNK, :] = (u[rows, :] * mixed).astype(BF16)


def _pre_call(x, mods, layer, mod_row, pk, cos, sin, tm):
    b, l, d = x.shape
    nt = l // tm
    per = tm // HALO
    last_halo = l // HALO - 1
    row_spec = lambda w: pl.BlockSpec((None, tm, w), lambda bi, i: (bi, i, 0))
    col_spec = lambda h: pl.BlockSpec((None, h, tm), lambda bi, i: (bi, 0, i))
    halo_prev = pl.BlockSpec((None, HALO, d), lambda bi, i: (bi, jnp.maximum(i * per - 1, 0), 0))
    halo_next = pl.BlockSpec((None, HALO, d), lambda bi, i: (bi, jnp.minimum((i + 1) * per, last_halo), 0))
    tab_spec = pl.BlockSpec((tm, LANE), lambda bi, i: (i, 0))
    weights = [pk[n] for n in ("g_pre_mix", "w_in", "g_q", "w_uq", "g_kv", "w_ukv")]
    tail_weights = [pk[n] for n in ("cm_norm_g", "cm_w", "cm_b", "ssd_conv_w", "ssd_conv_b")]
    outs = [((l, HEADS * QK_PAD), BF16, row_spec(HEADS * QK_PAD)), ((l, HEADS * QK_PAD), BF16, row_spec(HEADS * QK_PAD)),
            ((HEADS * VDIM, l), BF16, col_spec(HEADS * VDIM)), ((l, CM_WIDTH), BF16, row_spec(CM_WIDTH)),
            ((l, SSD_WIDTH), F32, row_spec(SSD_WIDTH)), ((l, SSD_WIDTH), F32, row_spec(SSD_WIDTH)),
            ((l, SSD_GROUPS * SSD_STATE), BF16, row_spec(SSD_GROUPS * SSD_STATE)),
            ((SSD_GROUPS * SSD_STATE, l), BF16, col_spec(SSD_GROUPS * SSD_STATE)), ((l, LANE), F32, row_spec(LANE))]
    return pl.pallas_call(
        _pre_kernel,
        out_shape=[jax.ShapeDtypeStruct((b,) + s, dt) for s, dt, _ in outs],
        grid=(b, nt),
        in_specs=[row_spec(d), halo_prev, halo_next, _mod_spec(mods, layer, mod_row)]
        + [_layer_spec(w, layer) for w in weights] + [tab_spec, tab_spec]
        + [_layer_spec(w, layer) for w in tail_weights],
        out_specs=[sp for _, _, sp in outs],
        scratch_shapes=[pltpu.VMEM((tm + 2 * HALO, d), BF16)],
        compiler_params=pltpu.CompilerParams(dimension_semantics=("arbitrary", "arbitrary"),
                                             vmem_limit_bytes=VMEM_LIMIT),
        name="pre_mix",
    )(x, x, x, mods, *weights, cos, sin, *tail_weights)


def _attn_kernel(q_ref, *refs, tk):
    o_ref, s_scr = refs[-2:]
    kv_refs = refs[:-2]
    tq = q_ref.shape[0]
    heads = q_ref.shape[1] // QK_PAD
    tiles = []
    for hh in range(heads):
        for seg in range(len(kv_refs) // 2):
            k_ref, vt_ref = kv_refs[2 * seg], kv_refs[2 * seg + 1]
            n = k_ref.shape[0]
            step = min(tk, n)
            tiles += [(hh, k_ref, vt_ref, start, step) for start in range(0, n, step)]

    def scores_into(tile, slot):
        hh, k_ref, _, start, step = tile
        s = _dot_nt(k_ref[start:start + step, hh * QK_PAD:(hh + 1) * QK_PAD],
                    q_ref[:, hh * QK_PAD:(hh + 1) * QK_PAD])
        s_scr[slot, 0:step, :] = s
        return jnp.max(s, axis=0, keepdims=True)

    mx = scores_into(tiles[0], 0)
    for t, (hh, _, vt_ref, start, step) in enumerate(tiles):
        slot = t % 2
        if t == 0 or tiles[t - 1][0] != hh:
            m = jnp.full((1, tq), -1e30, F32)
            acc = jnp.zeros((VDIM + ONES_ROWS, tq), F32)
        mx_next = scores_into(tiles[t + 1], 1 - slot) if t + 1 < len(tiles) else None
        m_new = jnp.maximum(m, mx)
        alpha = jnp.exp2(m - m_new)
        p = jnp.exp2(s_scr[slot, 0:step, :] - m_new).astype(BF16)
        vt_ext = jnp.concatenate([vt_ref[hh * VDIM:(hh + 1) * VDIM, start:start + step],
                                  jnp.ones((ONES_ROWS, step), BF16)], axis=0)
        acc = alpha * acc + _dot(vt_ext, p)
        m = m_new
        mx = mx_next
        if t + 1 == len(tiles) or tiles[t + 1][0] != hh:
            o_ref[:, hh * VDIM:(hh + 1) * VDIM] = (acc[0:VDIM, :] * (1.0 / acc[VDIM:VDIM + 1, :])).T.astype(BF16)


def _attn_call(q, kvs, tq, tk):
    b, lq, _ = q.shape
    hp = HEADS_PER_STEP
    in_specs = [pl.BlockSpec((None, tq, hp * QK_PAD), lambda bi, hi, i: (bi, i, hi))]
    args = [q]
    for k, vt in kvs:
        lk = k.shape[1]
        in_specs.append(pl.BlockSpec((None, lk, hp * QK_PAD), lambda bi, hi, i: (bi, 0, hi)))
        in_specs.append(pl.BlockSpec((None, hp * VDIM, lk), lambda bi, hi, i: (bi, hi, 0)))
        args += [k, vt]
    return pl.pallas_call(
        functools.partial(_attn_kernel, tk=tk),
        out_shape=jax.ShapeDtypeStruct((b, lq, HEADS * VDIM), BF16),
        grid=(b, HEADS // hp, lq // tq),
        in_specs=in_specs,
        out_specs=pl.BlockSpec((None, tq, hp * VDIM), lambda bi, hi, i: (bi, i, hi)),
        scratch_shapes=[pltpu.VMEM((2, tk, tq), F32)],
        compiler_params=pltpu.CompilerParams(dimension_semantics=("arbitrary", "arbitrary", "arbitrary"),
                                             vmem_limit_bytes=VMEM_LIMIT),
        name="mla_attention",
    )(*args)


def _ssd_kernel(xs_f, cm_f, bmt_f, dt_f, xs_b, cm_b_, bmt_b, dt_b,
                dtbias_ref, alog_ref, drep_ref, init_ref,
                yf_ref, yb_ref, fin_ref, st_ref, *, nc):
    j = pl.program_id(1)

    @pl.when(j == 0)
    def _():
        st_ref[...] = init_ref[...]

    nb = xs_f.shape[0]
    a_row = -jnp.exp(alog_ref[...])
    ri = lax.broadcasted_iota(jnp.int32, (CHUNK, CHUNK), 0)
    ci = lax.broadcasted_iota(jnp.int32, (CHUNK, CHUNK), 1)
    low_half = ci < (LANE // 2)
    causal = ((ri >= ci), (ri <= ci))
    tri = tuple(jnp.where(cz, 1.0, 0.0).astype(BF16) for cz in causal)
    streams = ((xs_f, cm_f, bmt_f, dt_f, yf_ref), (xs_b, cm_b_, bmt_b, dt_b, yb_ref))
    chains = [(n, d) for n in range(nb) for d in range(2)]
    groups = [(n, d, g) for (n, d) in chains for g in range(SSD_GROUPS)]

    dtv, a_parts = {}, {}
    for n, d in chains:
        dt_in = streams[d][3][n] + dtbias_ref[...]
        dtv[n, d] = jnp.maximum(dt_in, 0.0) + jnp.log1p(jnp.exp(-jnp.abs(dt_in)))
        a = dtv[n, d] * a_row
        a_hi = a.astype(BF16)
        r1 = a - a_hi.astype(F32)
        a_mid = r1.astype(BF16)
        a_parts[n, d] = (a_hi, a_mid, (r1 - a_mid.astype(F32)).astype(BF16))

    cum = {}
    for n, d in chains:
        a_hi, a_mid, a_lo = a_parts[n, d]
        cum[n, d] = _dot(tri[d], a_hi) + _dot(tri[d], a_mid) + _dot(tri[d], a_lo)

    cmb, gmat, y_off_raw, st_old = {}, {}, {}, {}
    for n, d, g in groups:
        gl = slice(g * LANE, (g + 1) * LANE)
        cmb[n, d, g] = streams[d][1][n, :, gl]
        gmat[n, d, g] = _dot(cmb[n, d, g], streams[d][2][n, g * SSD_STATE:(g + 1) * SSD_STATE, :])
        st_old[n, d, g] = st_ref[n, d, g]
        y_off_raw[n, d, g] = _dot(cmb[n, d, g], st_old[n, d, g].astype(BF16))

    cum_t, tot = {}, {}
    for n, d in chains:
        cum_t[n, d] = cum[n, d].T
        tot[n, d] = cum[n, d][CHUNK - 1:CHUNK, :] if d == 0 else cum[n, d][0:1, :]
    colx, totx, xdt_b, xdec_b, scores = {}, {}, {}, {}, {}
    for n, d, g in groups:
        i0 = d * SSD_HEADS + 2 * g
        i1 = i0 + 1
        gl = slice(g * LANE, (g + 1) * LANE)
        c_nd = cum[n, d]
        colx[n, d, g] = jnp.where(low_half, c_nd[:, i0:i0 + 1], c_nd[:, i1:i1 + 1])
        dtx = jnp.where(low_half, dtv[n, d][:, i0:i0 + 1], dtv[n, d][:, i1:i1 + 1])
        totx[n, d, g] = jnp.where(low_half[0:1, :], tot[n, d][:, i0:i0 + 1], tot[n, d][:, i1:i1 + 1])
        xdt = streams[d][0][n, :, gl] * dtx
        xdt_b[n, d, g] = xdt.astype(BF16)
        xdec_b[n, d, g] = (xdt * jnp.exp(totx[n, d, g] - colx[n, d, g])).astype(BF16)
        for hh, ii in enumerate((i0, i1)):
            diff = c_nd[:, ii:ii + 1] - cum_t[n, d][ii:ii + 1, :]
            lmat = jnp.exp(jnp.where(causal[d], diff, -1e30))
            scores[n, d, g, hh] = (gmat[n, d, g] * lmat).astype(BF16)

    ys, st_add = {}, {}
    for n, d, g in groups:
        ys[n, d, g] = [_dot(scores[n, d, g, hh], xdt_b[n, d, g]) for hh in range(2)]
        st_add[n, d, g] = _dot(streams[d][2][n, g * SSD_STATE:(g + 1) * SSD_STATE, :], xdec_b[n, d, g])

    for n, d, g in groups:
        gl = slice(g * LANE, (g + 1) * LANE)
        y = jnp.where(low_half, ys[n, d, g][0], ys[n, d, g][1]) + y_off_raw[n, d, g] * jnp.exp(colx[n, d, g])
        if d == 0:
            y = y + (drep_ref[0:1, gl] + drep_ref[1:2, gl]) * streams[d][0][n, :, gl]
        streams[d][4][n, :, gl] = y
        st_ref[n, d, g] = jnp.exp(totx[n, d, g]) * st_old[n, d, g] + st_add[n, d, g]

    @pl.when(j == nc - 1)
    def _():
        fin_ref[...] = st_ref[...]


def _ssd_call(xs, cmo, bmt, dt, pk, layer, init, nb_want=8):
    b, l, _ = xs.shape
    nc = l // CHUNK
    fwd = lambda bi, j: (bi, j, 0)
    bwd = lambda bi, j: (bi, nc - 1 - j, 0)
    fwd_t = lambda bi, j: (bi, 0, j)
    bwd_t = lambda bi, j: (bi, 0, nc - 1 - j)
    nb = nb_want if b % nb_want == 0 else 1
    rows = lambda w, f: pl.BlockSpec((nb, CHUNK, w), f)
    cols = lambda h, f: pl.BlockSpec((nb, h, CHUNK), f)
    small = [pk[n] for n in ("ssd_dt_bias", "ssd_a_log", "ssd_d")]
    st_shape = (2, SSD_GROUPS, SSD_STATE, LANE)
    st_spec = pl.BlockSpec((nb,) + st_shape, lambda bi, j: (bi, 0, 0, 0, 0))
    y_shape = jax.ShapeDtypeStruct((b, l, SSD_WIDTH), F32)
    per_dir = lambda f, ft: [rows(SSD_WIDTH, f), rows(SSD_GROUPS * SSD_STATE, f),
                             cols(SSD_GROUPS * SSD_STATE, ft), rows(LANE, f)]
    return pl.pallas_call(
        functools.partial(_ssd_kernel, nc=nc),
        out_shape=[y_shape, y_shape, jax.ShapeDtypeStruct((b,) + st_shape, F32)],
        grid=(b // nb, nc),
        in_specs=per_dir(fwd, fwd_t) + per_dir(bwd, bwd_t) + [_layer_spec(w, layer) for w in small] + [st_spec],
        out_specs=[rows(SSD_WIDTH, fwd), rows(SSD_WIDTH, bwd), st_spec],
        scratch_shapes=[pltpu.VMEM((nb,) + st_shape, F32)],
        compiler_params=pltpu.CompilerParams(dimension_semantics=("arbitrary", "arbitrary"),
                                             vmem_limit_bytes=VMEM_LIMIT),
        name="ssd_scan",
    )(xs, cmo, bmt, dt, xs, cmo, bmt, dt, *small, init)


def _post_kernel(x_ref, ya_ref, ycm_ref, yf_ref, yb_ref, z_ref, mod_ref,
                 gssd_ref, gpm_ref, gpf_ref, gpo_ref, wout_ref, w1_ref, w2_ref, o_ref, *, tf):
    d = x_ref.shape[-1]
    f = w1_ref.shape[-1]
    gate1 = mod_ref[:, 2 * d:3 * d]
    shift2 = mod_ref[:, 3 * d:4 * d]
    scale2 = mod_ref[:, 4 * d:5 * d]
    gate2 = mod_ref[:, 5 * d:6 * d]
    wa = HEADS * VDIM
    tm = x_ref.shape[0]
    sub = min(tm, SUB_ROWS)

    def mixed(r0):
        rs_ = slice(r0, r0 + sub)
        y_ssd = _rms((yf_ref[rs_, :] + yb_ref[rs_, :]) * _silu(z_ref[rs_, :]), gssd_ref[...]).astype(BF16)
        y = (_dot(ya_ref[rs_, :], wout_ref[0:wa, :]) + _dot(ycm_ref[rs_, :], wout_ref[wa:wa + CM_WIDTH, :])
             + _dot(y_ssd, wout_ref[wa + CM_WIDTH:, :]))
        x1 = x_ref[rs_, :] + gate1 * _rms(y, gpm_ref[...])
        return x1, (_rms(x1, gpf_ref[...]) * (1.0 + scale2) + shift2).astype(BF16)

    nxt = mixed(0)
    for r0 in range(0, tm, sub):
        rs_ = slice(r0, r0 + sub)
        x1, h2 = nxt
        if r0 + sub < tm:
            nxt = mixed(r0 + sub)
        acc = jnp.zeros(x1.shape, F32)
        for c in range(f // tf):
            hc = jnp.maximum(_dot(h2, w1_ref[:, c * tf:(c + 1) * tf]), 0.0)
            acc = acc + _dot((hc * hc).astype(BF16), w2_ref[c * tf:(c + 1) * tf, :])
        o_ref[rs_, :] = x1 + gate2 * _rms(acc, gpo_ref[...])


def _post_call(x, ya, ycm, yf, yb, z, mods, layer, mod_row, pk, tm):
    b, l, d = x.shape
    row_spec = lambda w: pl.BlockSpec((None, tm, w), lambda bi, i: (bi, i, 0))
    weights = [pk[n] for n in ("ssd_norm_g", "g_post_mix", "g_pre_ff", "g_post_ff", "w_out", "w_ff1", "w_ff2")]
    return pl.pallas_call(
        functools.partial(_post_kernel, tf=1024),
        out_shape=jax.ShapeDtypeStruct((b, l, d), F32),
        grid=(b, l // tm),
        in_specs=[row_spec(d), row_spec(HEADS * VDIM), row_spec(CM_WIDTH), row_spec(SSD_WIDTH), row_spec(SSD_WIDTH),
                  row_spec(SSD_WIDTH), _mod_spec(mods, layer, mod_row)]
        + [_layer_spec(w, layer, single=True) for w in weights],
        out_specs=row_spec(d),
        compiler_params=pltpu.CompilerParams(dimension_semantics=("arbitrary", "arbitrary"),
                                             vmem_limit_bytes=VMEM_LIMIT),
        name="post_mix_ffn",
    )(x, ya, ycm, yf, yb, z, mods, *weights)


def _rope_tables(n_tokens):
    rows_n = n_tokens // GRID_W
    axis_dim = ROPE // 2
    inv_freq = ROPE_BASE ** (-jnp.arange(0, axis_dim, 2, dtype=F32) / axis_dim)
    ang_r = jnp.arange(rows_n, dtype=F32)[:, None] * inv_freq
    ang_c = jnp.arange(GRID_W, dtype=F32)[:, None] * inv_freq
    per_row = lambda t: jnp.repeat(t, GRID_W, axis=0)
    per_col = lambda t: jnp.tile(t, (rows_n, 1))
    cos_r, sin_r = per_row(jnp.cos(ang_r)), per_row(jnp.sin(ang_r))
    cos_c, sin_c = per_col(jnp.cos(ang_c)), per_col(jnp.sin(ang_c))
    zeros = jnp.zeros((n_tokens, LANE - ROPE), F32)
    cos = jnp.concatenate([cos_r, cos_r, cos_c, cos_c, zeros], axis=-1)
    sin = jnp.concatenate([-sin_r, sin_r, -sin_c, sin_c, zeros], axis=-1)
    return cos, sin


def _rotate_half_partner():
    q = ROPE // 4
    return np.concatenate([np.arange(q, 2 * q), np.arange(0, q), np.arange(3 * q, 4 * q), np.arange(2 * q, 3 * q)])


def _pack_weights(p):
    depth = p["w_in"].shape[0]
    partner = _rotate_half_partner()
    o_kr = Q_LORA + KV_LORA
    o_cm = o_kr + ROPE
    o_dt = o_cm + 2 * CM_WIDTH + SSD_WIDTH + SSD_CONV_DIM
    w_in = p["w_in"].astype(BF16)
    kr = w_in[..., o_kr:o_cm]
    pad = jnp.zeros(w_in.shape[:-1] + (LANE - 2 * SSD_HEADS,), w_in.dtype)
    w_in_p = jnp.concatenate([w_in[..., :o_kr], kr, kr[..., partner], w_in[..., o_cm:o_dt], w_in[..., o_dt:], pad],
                             axis=-1)
    assert w_in_p.shape[-1] == P_END
    w_uq = p["w_uq"].astype(BF16).reshape(depth, Q_LORA, HEADS, NOPE + ROPE)
    rope_cols = w_uq[..., NOPE:]
    w_uq_p = jnp.concatenate([w_uq[..., :NOPE], rope_cols, rope_cols[..., partner]], axis=-1)
    row = lambda a: a.reshape(depth, 1, -1)
    pad_lane = lambda a: jnp.pad(a.reshape(depth, 1, -1), ((0, 0), (0, 0), (0, LANE - 2 * SSD_HEADS)))
    return {
        "g_pre_mix": row(p["g_pre_mix"]), "g_post_mix": row(p["g_post_mix"]),
        "g_pre_ff": row(p["g_pre_ff"]), "g_post_ff": row(p["g_post_ff"]),
        "w_in": w_in_p,
        "g_q": row(p["g_q"]) * QSCALE, "w_uq": w_uq_p.reshape(depth, Q_LORA, HEADS * QK_PAD),
        "g_kv": row(p["g_kv"]), "w_ukv": p["w_ukv"].astype(BF16),
        "cm_norm_g": row(p["cm_norm_g"]),
        "cm_w": jnp.transpose(p["cm_w_s"], (0, 2, 1, 3)).reshape(depth, CHUNK, CM_GROUPS * CHUNK).astype(BF16),
        "cm_b": jnp.repeat(jnp.transpose(p["cm_b_s"], (0, 2, 1)), CM_GROUP_DIM, axis=-1),
        "ssd_conv_w": p["ssd_conv_w"], "ssd_conv_b": row(p["ssd_conv_b"]),
        "ssd_dt_bias": pad_lane(p["ssd_dt_bias"]), "ssd_a_log": pad_lane(p["ssd_a_log"]),
        "ssd_d": jnp.repeat(p["ssd_d"], SSD_WIDTH // SSD_HEADS, axis=-1),
        "ssd_norm_g": row(p["ssd_norm_g"]),
        "w_out": p["w_out"].astype(BF16), "w_ff1": p["w_ff1"].astype(BF16), "w_ff2": p["w_ff2"].astype(BF16),
    }


def _row_tile(l, want):
    return want if l % want == 0 else l


def _tile_plan(n_lat, n_ctx):
    return _row_tile(n_lat, 4 * SUB_ROWS), _row_tile(n_ctx, SUB_ROWS), _row_tile(n_lat, 4 * SUB_ROWS), 512


def kernel(x, c, ctx, c_ctx, w_ada, b_ada, g_pre_mix, g_post_mix, g_pre_ff, g_post_ff, w_in, g_q, w_uq, g_kv, w_ukv, cm_norm_g, cm_w_s, cm_b_s, ssd_conv_w, ssd_conv_b, ssd_dt_bias, ssd_a_log, ssd_d, ssd_norm_g, w_out, w_ff1, w_ff2):
    b, n_lat, d = x.shape
    n_ctx = ctx.shape[1]
    depth = w_ada.shape[0]
    assert b + 1 <= MOD_ROWS and n_lat % CHUNK == 0 and n_ctx % CHUNK == 0 and n_lat % GRID_W == 0

    pk = _pack_weights(dict(
        g_pre_mix=g_pre_mix, g_post_mix=g_post_mix, g_pre_ff=g_pre_ff, g_post_ff=g_post_ff, w_in=w_in, g_q=g_q,
        w_uq=w_uq, g_kv=g_kv, w_ukv=w_ukv, cm_norm_g=cm_norm_g, cm_w_s=cm_w_s, cm_b_s=cm_b_s,
        ssd_conv_w=ssd_conv_w, ssd_conv_b=ssd_conv_b, ssd_dt_bias=ssd_dt_bias, ssd_a_log=ssd_a_log, ssd_d=ssd_d,
        ssd_norm_g=ssd_norm_g, w_out=w_out, w_ff1=w_ff1, w_ff2=w_ff2))

    cc = jnp.concatenate([c, c_ctx[None, :], jnp.zeros((MOD_ROWS - b - 1, d), c.dtype)], axis=0)
    mods = _mod_call(cc, w_ada, b_ada).reshape(depth, MOD_ROWS, 1, 6 * d)

    cos_l, sin_l = _rope_tables(n_lat)
    cos_c = jnp.concatenate([jnp.ones((n_ctx, ROPE), F32), jnp.zeros((n_ctx, LANE - ROPE), F32)], axis=-1)
    sin_c = jnp.zeros((n_ctx, LANE), F32)

    tm_l, tm_c, tq_l, tk = _tile_plan(n_lat, n_ctx)
    zero_states = jnp.zeros((b, 2, SSD_GROUPS, SSD_STATE, LANE), F32)

    xc = ctx
    for layer in range(depth):
        last = layer == depth - 1

        cq, ck, cv, cycm, cz, cxs, ccm, cbmt, cdt = _pre_call(xc, mods, layer, b, pk, cos_c, sin_c, tm_c)
        cyf, cyb, ctx_states = _ssd_call(cxs, ccm, cbmt, cdt, pk, layer, zero_states)
        if not last:
            cya = _attn_call(cq, [(ck, cv)], tm_c, tk)
            xc = _post_call(xc, cya, cycm, cyf, cyb, cz, mods, layer, b, pk, tm_c)

        q, k, v, ycm, z, xs, cmo, bmt, dt = _pre_call(x, mods, layer, None, pk, cos_l, sin_l, tm_l)
        yf, yb, _ = _ssd_call(xs, cmo, bmt, dt, pk, layer, ctx_states)
        ya = _attn_call(q, [(k, v), (ck, cv)], tq_l, tk)
        x = _post_call(x, ya, ycm, yf, yb, z, mods, layer, None, pk, tm_l)
    return x
```
